```python
import jax, jax.numpy as jnp
from jax import lax
import numpy as np

D_MODEL = 1024
BATCH = 16
SEQ = 4096
DEPTH = 4

CTX_LEN = 256
GRID_W = 64
HEAD_DIM = 64
D_CONV = 256
CONV_WIDTH = 31
N_DN_HEADS = 6
D_DN = N_DN_HEADS * HEAD_DIM
SHORT_CONV = 5
DN_CHUNK = 64
N_Q_HEADS = 6
N_KV_HEADS = 2
D_ATTN = N_Q_HEADS * HEAD_DIM
D_KV = N_KV_HEADS * HEAD_DIM
WINDOW = 128
ATTN_BLOCK = 128
ROPE_BASE = 10000.0
D_MIX = D_CONV + D_DN + D_ATTN
N_IN = 3 * D_CONV + 4 * D_DN + 4 * N_DN_HEADS + 2 * D_ATTN + 2 * D_KV
EPS = 1e-6

kernel_name = "hymba_style_conv_deltanet_swa_dit"


def _rmsnorm(x, w):
    xf = x.astype(jnp.float32)
    y = xf * lax.rsqrt(jnp.mean(xf * xf, axis=-1, keepdims=True) + EPS)
    return (y * w.astype(jnp.float32)).astype(x.dtype)


def _layernorm(x, w, b):
    xf = x.astype(jnp.float32)
    mu = jnp.mean(xf, axis=-1, keepdims=True)
    var = jnp.mean(jnp.square(xf - mu), axis=-1, keepdims=True)
    y = (xf - mu) * lax.rsqrt(var + EPS) * w.astype(jnp.float32) + b.astype(jnp.float32)
    return y.astype(x.dtype)


def _l2norm(x):
    return x * lax.rsqrt(jnp.sum(x * x, axis=-1, keepdims=True) + EPS)


def _depthwise_conv(x, w):
    k = w.shape[0]
    pad = (k - 1) // 2
    return lax.conv_general_dilated(
        x, w[:, None, :].astype(x.dtype), window_strides=(1,), padding=[(pad, k - 1 - pad)],
        dimension_numbers=('NWC', 'WIO', 'NWC'), feature_group_count=x.shape[-1])


def _ada(cond, w, b):
    m = jax.nn.silu(cond) @ w + b
    return jnp.split(m, 3, axis=-1)


def _split_cols(p):
    sizes = (D_CONV, D_CONV, D_CONV,
             D_DN, D_DN, D_DN, D_DN,
             4 * N_DN_HEADS,
             D_ATTN, D_KV, D_KV, D_ATTN)
    idx = np.cumsum(sizes)[:-1].tolist()
    return jnp.split(p, idx, axis=-1)


def _axial_rope(t, rows, cols):
    d = t.shape[-1]
    half, quarter = d // 2, d // 4
    inv = ROPE_BASE ** (-jnp.arange(quarter, dtype=jnp.float32) / quarter)

    def rot(th, pos):
        ang = pos.astype(jnp.float32)[:, None] * inv
        cos = jnp.cos(ang)[None, :, None, :].astype(t.dtype)
        sin = jnp.sin(ang)[None, :, None, :].astype(t.dtype)
        x1, x2 = th[..., :quarter], th[..., quarter:]
        return jnp.concatenate([x1 * cos - x2 * sin, x2 * cos + x1 * sin], axis=-1)

    return jnp.concatenate([rot(t[..., :half], rows), rot(t[..., half:], cols)], axis=-1)


def _conformer_conv(a_val, a_gate, z, conv_w, conv_b, ln_w, ln_b):
    u = a_val * jax.nn.sigmoid(a_gate)
    u = _depthwise_conv(u, conv_w) + conv_b
    u = jax.nn.silu(_layernorm(u, ln_w, ln_b))
    return u * jax.nn.silu(z)


def _gdn_inputs(q, k, v, sc, conv_w, a_log, dt_bias):
    b_, l_ = q.shape[:2]
    qkv = jax.nn.silu(_depthwise_conv(jnp.concatenate([q, k, v], axis=-1), conv_w)).astype(jnp.float32)
    q, k, v = jnp.split(qkv, 3, axis=-1)
    q = _l2norm(q.reshape(b_, l_, N_DN_HEADS, HEAD_DIM))
    k = _l2norm(k.reshape(b_, l_, N_DN_HEADS, HEAD_DIM))
    v = v.reshape(b_, l_, N_DN_HEADS, HEAD_DIM)
    sc = sc.astype(jnp.float32).reshape(b_, l_, 2, 2, N_DN_HEADS)
    beta = jax.nn.sigmoid(sc[:, :, :, 0])
    g = -jnp.exp(a_log.astype(jnp.float32)) * jax.nn.softplus(sc[:, :, :, 1] + dt_bias.astype(jnp.float32))
    return q, k, v, g, beta


def _gdn_chunk_scan(q, k, v, g, beta, s0, with_out):
    b_, l_, h_, dk = q.shape
    dv = v.shape[-1]
    n = l_ // DN_CHUNK

    def chunked(t):
        t = t.reshape((b_, n, DN_CHUNK, h_) + t.shape[3:])
        return jnp.moveaxis(jnp.moveaxis(t, 1, 0), 3, 2)

    qc = chunked(q) * dk ** -0.5
    kc, vc = chunked(k), chunked(v)
    bc = chunked(beta)
    gc = jnp.cumsum(chunked(g), axis=-1)
    idx = jnp.arange(DN_CHUNK)
    strict = idx[:, None] > idx[None, :]
    incl = idx[:, None] >= idx[None, :]
    diff = gc[..., :, None] - gc[..., None, :]
    kb = kc * bc[..., None]
    lmat = jnp.einsum('nbhid,nbhjd->nbhij', kb, kc) * jnp.exp(jnp.where(strict, diff, -jnp.inf))
    amat = lmat + jnp.eye(DN_CHUNK, dtype=lmat.dtype)
    rhs = jnp.concatenate([vc * bc[..., None], kb * jnp.exp(gc)[..., None]], axis=-1)
    sol = lax.linalg.triangular_solve(amat, rhs, left_side=True, lower=True, unit_diagonal=True)
    u, w = sol[..., :dv], sol[..., dv:]
    g_last = gc[..., -1]
    k_dec = kc * jnp.exp(g_last[..., None] - gc)[..., None]
    xs = (u, w, k_dec, g_last)
    if with_out:
        intra = jnp.einsum('nbhid,nbhjd->nbhij', qc, kc) * jnp.exp(jnp.where(incl, diff, -jnp.inf))
        q_dec = qc * jnp.exp(gc)[..., None]
        xs = xs + (q_dec, intra)

    def step(s, xs_i):
        u_i, w_i, kd_i, gl_i = xs_i[:4]
        v_new = u_i - jnp.einsum('bhck,bhkv->bhcv', w_i, s)
        s_new = s * jnp.exp(gl_i)[..., None, None] + jnp.einsum('bhck,bhcv->bhkv', kd_i, v_new)
        if with_out:
            qd_i, a_i = xs_i[4:]
            o_i = jnp.einsum('bhck,bhkv->bhcv', qd_i, s) + jnp.einsum('bhij,bhjv->bhiv', a_i, v_new)
            return s_new, o_i
        return s_new, None

    s_fin, o = lax.scan(step, s0, xs)
    if with_out:
        o = jnp.moveaxis(jnp.moveaxis(o, 2, 3), 0, 1).reshape(b_, l_, h_, dv)
    return o, s_fin


def _gdn_bidir(q, k, v, g, beta, s0_f, s0_b, with_out):
    flip = lambda t: jnp.flip(t, axis=1)
    o_f, s_f = _gdn_chunk_scan(q, k, v, g[:, :, 0], beta[:, :, 0], s0_f, with_out)
    o_b, s_b = _gdn_chunk_scan(flip(q), flip(k), flip(v), flip(g[:, :, 1]), flip(beta[:, :, 1]), s0_b, with_out)
    o = o_f + flip(o_b) if with_out else None
    return o, s_f, s_b


def _gated_head_norm(o, z, w, dtype):
    b_, l_ = o.shape[:2]
    return _rmsnorm(o.astype(dtype), w).reshape(b_, l_, D_DN) * jax.nn.silu(z)


def _softmax_with_sink(scores, sink):
    s = jnp.broadcast_to(sink[None, :, :, None, None], scores.shape[:-1] + (1,))
    p = jax.nn.softmax(jnp.concatenate([scores, s], axis=-1), axis=-1)
    return p[..., :-1]


def _latent_attention(q, k, v, k_ctx, v_ctx, sink):
    b_, l_, hq, d = q.shape
    group = hq // N_KV_HEADS
    nb = l_ // ATTN_BLOCK
    span = ATTN_BLOCK + 2 * WINDOW
    scale = d ** -0.5
    qg = q.reshape(b_, nb, ATTN_BLOCK, N_KV_HEADS, group, d)
    pad = ((0, 0), (WINDOW, WINDOW), (0, 0), (0, 0))
    kp, vp = jnp.pad(k, pad), jnp.pad(v, pad)
    offs_q = jnp.arange(ATTN_BLOCK)
    offs_k = jnp.arange(span) - WINDOW
    band = jnp.abs(offs_k[None, :] - offs_q[:, None]) <= WINDOW
    sink_g = sink.reshape(N_KV_HEADS, group).astype(jnp.float32)

    def one_block(i):
        q_i = qg[:, i]
        k_i = lax.dynamic_slice_in_dim(kp, i * ATTN_BLOCK, span, axis=1)
        v_i = lax.dynamic_slice_in_dim(vp, i * ATTN_BLOCK, span, axis=1)
        kpos = i * ATTN_BLOCK + offs_k
        valid = band & ((kpos >= 0) & (kpos < l_))[None, :]
        s_loc = jnp.einsum('bqhgd,bkhd->bhgqk', q_i, k_i).astype(jnp.float32) * scale
        s_loc = jnp.where(valid, s_loc, -jnp.inf)
        s_ctx = jnp.einsum('bqhgd,bkhd->bhgqk', q_i, k_ctx).astype(jnp.float32) * scale
        p = _softmax_with_sink(jnp.concatenate([s_loc, s_ctx], axis=-1), sink_g).astype(v.dtype)
        return (jnp.einsum('bhgqk,bkhd->bqhgd', p[..., :span], v_i)
                + jnp.einsum('bhgqk,bkhd->bqhgd', p[..., span:], v_ctx))

    out = lax.map(one_block, jnp.arange(nb))
    return jnp.moveaxis(out, 0, 1).reshape(b_, l_, hq * d)


def _context_attention(q, k, v, sink):
    b_, lc, hq, d = q.shape
    group = hq // N_KV_HEADS
    qg = q.reshape(b_, lc, N_KV_HEADS, group, d)
    s = jnp.einsum('bqhgd,bkhd->bhgqk', qg, k).astype(jnp.float32) * d ** -0.5
    p = _softmax_with_sink(s, sink.reshape(N_KV_HEADS, group).astype(jnp.float32)).astype(v.dtype)
    return jnp.einsum('bhgqk,bkhd->bqhgd', p, v).reshape(b_, lc, hq * d)


def _hybrid_layer(x, xc, c3, cc3, rows, cols, norm_w, ada_w, ada_b, w_in, conv_a_w, conv_a_b,
                  ln_a_w, ln_a_b, qkv_conv_w, a_log, dt_bias, dn_norm_w, sink, w_out, with_ctx_out):
    b_, l_ = x.shape[:2]
    lc = xc.shape[1]
    shift, scale, gate = _ada(c3, ada_w, ada_b)
    shift_c, scale_c, gate_c = _ada(cc3, ada_w, ada_b)
    h = _rmsnorm(x, norm_w) * (1.0 + scale) + shift
    hc = _rmsnorm(xc, norm_w) * (1.0 + scale_c) + shift_c
    (a_val, a_gate, z_a, q_b, k_b, v_b, z_b, sc_b, q_c, k_c, v_c, z_c) = _split_cols(h @ w_in)
    (a_val_x, a_gate_x, z_a_x, q_b_x, k_b_x, v_b_x, z_b_x, sc_b_x,
     q_c_x, k_c_x, v_c_x, z_c_x) = _split_cols(hc @ w_in)

    y_a = _conformer_conv(a_val, a_gate, z_a, conv_a_w, conv_a_b, ln_a_w, ln_a_b)

    s0 = jnp.zeros((b_, N_DN_HEADS, HEAD_DIM, HEAD_DIM), jnp.float32)
    gin_x = _gdn_inputs(q_b_x, k_b_x, v_b_x, sc_b_x, qkv_conv_w, a_log, dt_bias)
    o_bx, s_f, s_b = _gdn_bidir(*gin_x, s0, s0, with_ctx_out)
    gin = _gdn_inputs(q_b, k_b, v_b, sc_b, qkv_conv_w, a_log, dt_bias)
    o_b, _, _ = _gdn_bidir(*gin, s_f, s_b, True)
    y_b = _gated_head_norm(o_b, z_b, dn_norm_w, x.dtype)

    k_ctx = k_c_x.reshape(b_, lc, N_KV_HEADS, HEAD_DIM)
    v_ctx = v_c_x.reshape(b_, lc, N_KV_HEADS, HEAD_DIM)
    q_lat = _axial_rope(q_c.reshape(b_, l_, N_Q_HEADS, HEAD_DIM), rows, cols)
    k_lat = _axial_rope(k_c.reshape(b_, l_, N_KV_HEADS, HEAD_DIM), rows, cols)
    v_lat = v_c.reshape(b_, l_, N_KV_HEADS, HEAD_DIM)
    y_c = _latent_attention(q_lat, k_lat, v_lat, k_ctx, v_ctx, sink) * jax.nn.silu(z_c)

    y = jnp.concatenate([y_a, y_b, y_c], axis=-1) @ w_out
    x = x + gate * y

    if with_ctx_out:
        y_ax = _conformer_conv(a_val_x, a_gate_x, z_a_x, conv_a_w, conv_a_b, ln_a_w, ln_a_b)
        y_bx = _gated_head_norm(o_bx, z_b_x, dn_norm_w, xc.dtype)
        q_ctx = q_c_x.reshape(b_, lc, N_Q_HEADS, HEAD_DIM)
        y_cx = _context_attention(q_ctx, k_ctx, v_ctx, sink) * jax.nn.silu(z_c_x)
        yc = jnp.concatenate([y_ax, y_bx, y_cx], axis=-1) @ w_out
        xc = xc + gate_c * yc
    return x, xc


def setup_inputs(seed: int = 0) -> dict:
    key = jax.random.key(seed)
    ks = jax.random.split(key, 20)
    f32 = jnp.float32
    nrm = lambda k, s: jax.random.normal(k, s, f32)
    dt = jnp.exp(jax.random.uniform(ks[12], (DEPTH, 2, N_DN_HEADS), f32)
                 * (np.log(0.1) - np.log(0.001)) + np.log(0.001))
    return {
        "x": nrm(ks[0], (BATCH, SEQ, D_MODEL)),
        "c": nrm(ks[1], (BATCH, D_MODEL)),
        "ctx": nrm(ks[2], (BATCH, CTX_LEN, D_MODEL)),
        "c_ctx": nrm(ks[3], (D_MODEL,)),
        "norm_w": 1.0 + 0.1 * nrm(ks[4], (DEPTH, D_MODEL)),
        "ada_w": nrm(ks[5], (DEPTH, D_MODEL, 3 * D_MODEL)) * (0.5 * D_MODEL ** -0.5),
        "ada_b": 0.02 * nrm(ks[6], (DEPTH, 3 * D_MODEL)),
        "w_in": nrm(ks[7], (DEPTH, D_MODEL, N_IN)) * D_MODEL ** -0.5,
        "conv_a_w": nrm(ks[8], (DEPTH, CONV_WIDTH, D_CONV)) * CONV_WIDTH ** -0.5,
        "conv_a_b": 0.02 * nrm(ks[9], (DEPTH, D_CONV)),
        "ln_a_w": 1.0 + 0.1 * nrm(ks[10], (DEPTH, D_CONV)),
        "ln_a_b": 0.02 * nrm(ks[11], (DEPTH, D_CONV)),
        "qkv_conv_w": nrm(ks[13], (DEPTH, SHORT_CONV, 3 * D_DN)) * SHORT_CONV ** -0.5,
        "a_log": jnp.log(jax.random.uniform(ks[14], (DEPTH, 2, N_DN_HEADS), f32, 1.0, 16.0)),
        "dt_bias": dt + jnp.log(-jnp.expm1(-dt)),
        "dn_norm_w": 1.0 + 0.1 * nrm(ks[15], (DEPTH, HEAD_DIM)),
        "sink": nrm(ks[16], (DEPTH, N_Q_HEADS)),
        "w_out": nrm(ks[17], (DEPTH, D_MIX, D_MODEL)) * D_MIX ** -0.5,
        "final_norm_w": 1.0 + 0.1 * nrm(ks[18], (D_MODEL,)),
    }


def reference(x, c, ctx, c_ctx, norm_w, ada_w, ada_b, w_in, conv_a_w, conv_a_b, ln_a_w, ln_a_b,
              qkv_conv_w, a_log, dt_bias, dn_norm_w, sink, w_out, final_norm_w):
    n_tok = x.shape[1]
    n_rows = n_tok // GRID_W
    rows = jnp.repeat(jnp.arange(n_rows, dtype=jnp.int32), GRID_W)
    cols = jnp.tile(jnp.arange(GRID_W, dtype=jnp.int32), n_rows)
    c3 = c[:, None, :]
    cc3 = c_ctx[None, None, :]
    xc = ctx
    for l in range(DEPTH):
        x, xc = _hybrid_layer(x, xc, c3, cc3, rows, cols, norm_w[l], ada_w[l], ada_b[l], w_in[l],
                              conv_a_w[l], conv_a_b[l], ln_a_w[l], ln_a_b[l], qkv_conv_w[l],
                              a_log[l], dt_bias[l], dn_norm_w[l], sink[l], w_out[l],
                              l < DEPTH - 1)
    return _rmsnorm(x, final_norm_w)
```

```python
import functools

import jax
import jax.numpy as jnp
import numpy as np
from jax import lax
from jax.experimental import pallas as pl
from jax.experimental.pallas import tpu as pltpu

F32 = jnp.float32
BF16 = jnp.bfloat16

HEAD_DIM = 64
LANES = 128
D_CONV = 256
CONV_WIDTH = 31
N_DN_HEADS = 6
D_DN = N_DN_HEADS * HEAD_DIM
SHORT_CONV = 5
DN_CHUNK = 64
N_Q_HEADS = 6
N_KV_HEADS = 2
D_ATTN = N_Q_HEADS * HEAD_DIM
D_KV = N_KV_HEADS * HEAD_DIM
WINDOW = 128
ATTN_BLOCK = 128
ROPE_BASE = 10000.0
GRID_W = 64
EPS = 1e-6
N_PAIRS = N_DN_HEADS // 2
HALO = 16
NEG = -1e30
VMEM_LIMIT = 48 * 1024 * 1024

N_A = 3 * D_CONV
N_QKV = 3 * D_DN
N_SC = 4 * N_DN_HEADS
N_KVC = 2 * D_KV
N_ZZ = D_DN + D_ATTN
OFF_QKV = N_A
OFF_SC = OFF_QKV + N_QKV
OFF_QC = OFF_SC + LANES
OFF_KVC = OFF_QC + D_ATTN
OFF_ZZ = OFF_KVC + N_KVC
N_IN_PAD = OFF_ZZ + N_ZZ
Q_PERM = (0, 3, 1, 4, 2, 5)


def _silu(x):
    return x * jax.nn.sigmoid(x)


def _mm(a, b):
    return jnp.dot(a.astype(BF16), b.astype(BF16), preferred_element_type=F32)


def _mm_nt(a, b):
    return lax.dot_general(a.astype(BF16), b.astype(BF16), (((1,), (1,)), ((), ())), preferred_element_type=F32)


def _mm_tn(a, b):
    return lax.dot_general(a.astype(BF16), b.astype(BF16), (((0,), (0,)), ((), ())), preferred_element_type=F32)


def _split2(x):
    hi = x.astype(BF16)
    lo = (x - hi.astype(F32)).astype(BF16)
    return hi, lo


def _mm_x2(a, b01):
    hi, lo = _split2(a)
    return (jnp.dot(hi, b01, preferred_element_type=F32) + jnp.dot(lo, b01, preferred_element_type=F32))


def _head_sums(x):
    r = lax.broadcasted_iota(jnp.int32, (LANES, LANES), 0) // HEAD_DIM
    c = lax.broadcasted_iota(jnp.int32, (LANES, LANES), 1) // HEAD_DIM
    ones_bd = jnp.where(r == c, 1.0, 0.0).astype(BF16)
    return _mm_x2(x, ones_bd)


def _ada_body(cond_ref, w_ref, b_ref, o_ref):
    a = _silu(cond_ref[...])
    o_ref[0] = jnp.dot(a, w_ref[0], preferred_element_type=F32, precision=lax.Precision.HIGHEST) + b_ref[0]


def _ada_call(cond, ada_w, ada_b):
    depth, d, d3 = ada_w.shape
    rows = cond.shape[0]
    tn = 1024
    return pl.pallas_call(
        _ada_body,
        grid=(depth, d3 // tn),
        in_specs=[pl.BlockSpec((rows, d), lambda l, j: (0, 0)),
                  pl.BlockSpec((1, d, tn), lambda l, j: (l, 0, j)),
                  pl.BlockSpec((1, 1, tn), lambda l, j: (l, 0, j))],
        out_specs=pl.BlockSpec((1, rows, tn), lambda l, j: (l, 0, j)),
        out_shape=jax.ShapeDtypeStruct((depth, rows, d3), F32),
        compiler_params=pltpu.CompilerParams(vmem_limit_bytes=VMEM_LIMIT),
        name="ada",
    )(cond, ada_w, ada_b.reshape(depth, 1, d3))


def _inproj_body(x_ref, mod_ref, nw_ref, w_ref, lanep_ref, cos_ref, sin_ref,
                 pa_ref, pqkv_ref, gb_ref, qc_ref, kvc_ref, zz_ref, *, rope):
    x = x_ref[0]
    y = x * lax.rsqrt(jnp.mean(x * x, axis=-1, keepdims=True) + EPS) * nw_ref[...]
    h = (y * (1.0 + mod_ref[0, 1:2, :]) + mod_ref[0, 0:1, :]).astype(BF16)

    def proj(off, n):
        return jnp.dot(h, w_ref[:, off:off + n], preferred_element_type=F32)

    pa_ref[0] = proj(0, N_A)
    pqkv_ref[0] = proj(OFF_QKV, N_QKV)
    zz_ref[0] = proj(OFF_ZZ, N_ZZ)

    s = proj(OFF_SC, LANES)
    a_log, dt_b, is_decay = lanep_ref[0:1, :], lanep_ref[1:2, :], lanep_ref[2:3, :] > 0.5
    t = s + dt_b
    softplus = jnp.maximum(t, 0.0) + jnp.log(1.0 + jnp.exp(-jnp.abs(t)))
    gb_ref[0] = jnp.where(is_decay, -jnp.exp(a_log) * softplus, jax.nn.sigmoid(s))

    qk = proj(OFF_QC, D_ATTN + D_KV)
    v = proj(OFF_KVC + D_KV, D_KV)
    lane = lax.broadcasted_iota(jnp.int32, (1, LANES), 1)
    first = (lane % (HEAD_DIM // 2)) < (HEAD_DIM // 4)
    for t_ in range((D_ATTN + D_KV) // LANES):
        tile = qk[:, t_ * LANES:(t_ + 1) * LANES]
        if rope:
            partner = jnp.where(first, pltpu.roll(tile, LANES - HEAD_DIM // 4, 1), pltpu.roll(tile, HEAD_DIM // 4, 1))
            tile = tile * cos_ref[...] + partner * sin_ref[...]
        if t_ < D_ATTN // LANES:
            qc_ref[0, :, t_ * LANES:(t_ + 1) * LANES] = tile * (HEAD_DIM ** -0.5)
        else:
            kvc_ref[0, :, 0:D_KV] = tile
    kvc_ref[0, :, D_KV:] = v


def _inproj_call(x, mod3, norm_w, w_p, lanep, cos_t, sin_t, *, rope, tm):
    b, l, d = x.shape
    outs = (N_A, N_QKV, LANES, D_ATTN, N_KVC, N_ZZ)
    tok = lambda n: pl.BlockSpec((1, tm, n), lambda bi, i: (bi, i, 0))
    return pl.pallas_call(
        functools.partial(_inproj_body, rope=rope),
        grid=(b, l // tm),
        in_specs=[tok(d),
                  pl.BlockSpec((1, 3, d), lambda bi, i: (bi, 0, 0)),
                  pl.BlockSpec((1, d), lambda bi, i: (0, 0)),
                  pl.BlockSpec((d, N_IN_PAD), lambda bi, i: (0, 0)),
                  pl.BlockSpec((8, LANES), lambda bi, i: (0, 0)),
                  pl.BlockSpec((tm, LANES), lambda bi, i: (i, 0)),
                  pl.BlockSpec((tm, LANES), lambda bi, i: (i, 0))],
        out_specs=[tok(n) for n in outs],
        out_shape=[jax.ShapeDtypeStruct((b, l, n), F32) for n in outs],
        compiler_params=pltpu.CompilerParams(dimension_semantics=("parallel", "parallel"),
                                             vmem_limit_bytes=VMEM_LIMIT),
        name="inproj_rope" if rope else "inproj",
    )(x, mod3, norm_w, w_p, lanep, cos_t, sin_t)


def _fill_ext(ext_ref, main, prev, nxt, i, n_tiles, tile):
    ext_ref[0:HALO, :] = jnp.where(i > 0, prev, 0.0)
    ext_ref[HALO:HALO + tile, :] = main
    ext_ref[HALO + tile:, :] = jnp.where(i < n_tiles - 1, nxt, 0.0)


def _dwconv(ext_ref, w_ref, width, rows, r0):
    pad = (width - 1) // 2
    acc = None
    for k in range(width):
        term = ext_ref[r0 + HALO - pad + k:r0 + HALO - pad + k + rows, :] * w_ref[k:k + 1, :]
        acc = term if acc is None else acc + term
    return acc


def _mixprep_body(pa_ref, pap_ref, pan_ref, q_ref, qp_ref, qn_ref, caw_ref, cab_ref, lnw_ref, lnb_ref, qcw_ref,
                  ya_ref, qkvn_ref, exta_ref, extq_ref, *, tile, n_tiles):
    i = pl.program_id(1)
    glu = lambda t: t[:, 0:D_CONV] * jax.nn.sigmoid(t[:, D_CONV:2 * D_CONV])
    _fill_ext(exta_ref, glu(pa_ref[0]), glu(pap_ref[0]), glu(pan_ref[0]), i, n_tiles, tile)
    _fill_ext(extq_ref, q_ref[0], qp_ref[0], qn_ref[0], i, n_tiles, tile)
    sub = 32
    for r0 in range(0, tile, sub):
        u = _dwconv(exta_ref, caw_ref, CONV_WIDTH, sub, r0) + cab_ref[...]
        mu = jnp.mean(u, axis=-1, keepdims=True)
        var = jnp.mean(jnp.square(u - mu), axis=-1, keepdims=True)
        u = (u - mu) * lax.rsqrt(var + EPS) * lnw_ref[...] + lnb_ref[...]
        ya_ref[0, r0:r0 + sub, :] = _silu(u) * _silu(pa_ref[0, r0:r0 + sub, 2 * D_CONV:])
    for r0 in range(0, tile, sub):
        t = _silu(_dwconv(extq_ref, qcw_ref, SHORT_CONV, sub, r0))
        for c0 in range(0, N_QKV, LANES):
            tl = t[:, c0:c0 + LANES]
            if c0 < 2 * D_DN:
                tl = tl * lax.rsqrt(_head_sums(tl * tl) + EPS)
                if c0 < D_DN:
                    tl = tl * (HEAD_DIM ** -0.5)
            qkvn_ref[0, r0:r0 + sub, c0:c0 + LANES] = tl


def _mixprep_call(pa, pqkv, conv_a_w, conv_a_b, ln_w, ln_b, qkv_conv_w, *, tile):
    b, l, _ = pa.shape
    n_tiles = l // tile
    hpt = tile // HALO
    n_h = l // HALO
    main = lambda n: pl.BlockSpec((1, tile, n), lambda bi, i: (bi, i, 0))
    prev = lambda n: pl.BlockSpec((1, HALO, n), lambda bi, i: (bi, jnp.maximum(i * hpt - 1, 0), 0))
    nxt = lambda n: pl.BlockSpec((1, HALO, n), lambda bi, i: (bi, jnp.minimum((i + 1) * hpt, n_h - 1), 0))
    full = lambda a: pl.BlockSpec(a.shape, lambda bi, i: (0, 0))
    consts = (conv_a_w, conv_a_b.reshape(1, -1), ln_w.reshape(1, -1), ln_b.reshape(1, -1), qkv_conv_w)
    return pl.pallas_call(
        functools.partial(_mixprep_body, tile=tile, n_tiles=n_tiles),
        grid=(b, n_tiles),
        in_specs=[main(N_A), prev(N_A), nxt(N_A), main(N_QKV), prev(N_QKV), nxt(N_QKV)] + [full(a) for a in consts],
        out_specs=[main(D_CONV), main(N_QKV)],
        out_shape=[jax.ShapeDtypeStruct((b, l, D_CONV), F32), jax.ShapeDtypeStruct((b, l, N_QKV), F32)],
        scratch_shapes=[pltpu.VMEM((tile + 2 * HALO, D_CONV), F32), pltpu.VMEM((tile + 2 * HALO, N_QKV), F32)],
        compiler_params=pltpu.CompilerParams(dimension_semantics=("parallel", "parallel"),
                                             vmem_limit_bytes=VMEM_LIMIT),
        name="mixprep",
    )(pa, pa, pa, pqkv, pqkv, pqkv, *consts)


def _blockdiag(x2, bm):
    return jnp.where(bm, jnp.concatenate([x2, x2], axis=0), 0.0)


def _gdn_dir(d, qkv, gb, e_ref, s_ref, o_ref):
    c = DN_CHUNK
    ri = lax.broadcasted_iota(jnp.int32, (c, LANES), 0)
    cj = lax.broadcasted_iota(jnp.int32, (c, LANES), 1) % HEAD_DIM
    incl = (ri >= cj) if d == 0 else (ri <= cj)
    strict = (ri > cj) if d == 0 else (ri < cj)
    blk16 = (ri // 16) == (cj // 16)
    blk32 = (ri // 32) == (cj // 32)
    bm =(lax.broadcasted_iota(jnp.int32, (LANES, LANES), 0) // HEAD_DIM
          == lax.broadcasted_iota(jnp.int32, (LANES, LANES), 1) // HEAD_DIM)
    ti = lax.broadcasted_iota(jnp.int32, (c, c), 0)
    tt = lax.broadcasted_iota(jnp.int32, (c, c), 1)
    tri = jnp.where((tt <= ti) if d == 0 else (tt >= ti), 1.0, 0.0).astype(BF16)
    tri_t2 = jnp.where((ri <= cj) if d == 0 else (ri >= cj), 1.0, 0.0).astype(BF16)
    g_hi, g_lo = _split2(gb)
    csum = jnp.dot(tri, g_hi, preferred_element_type=F32) + jnp.dot(tri, g_lo, preferred_element_type=F32)
    tn = (((0,), (0,)), ((), ()))
    crow = (lax.dot_general(g_hi, tri_t2, tn, preferred_element_type=F32)
            + lax.dot_general(g_lo, tri_t2, tn, preferred_element_type=F32))
    feat = lax.broadcasted_iota(jnp.int32, (1, LANES), 1)
    is_decay = (feat % (2 * N_DN_HEADS)) >= N_DN_HEADS
    src_hi, src_lo = _split2(jnp.where(is_decay, csum, gb))
    lane_lo = lax.broadcasted_iota(jnp.int32, (1, LANES), 1) < HEAD_DIM

    for p in range(N_PAIRS):
        q2 = qkv[:, p * LANES:(p + 1) * LANES]
        k2 = qkv[:, D_DN + p * LANES:D_DN + (p + 1) * LANES]
        v2 = qkv[:, 2 * D_DN + p * LANES:2 * D_DN + (p + 1) * LANES]
        sel = e_ref[d, p]
        bc = jnp.dot(src_hi, sel, preferred_element_type=F32) + jnp.dot(src_lo, sel, preferred_element_type=F32)
        beta2, c2 = bc[:, :LANES], bc[:, LANES:]
        f0 = d * 2 * N_DN_HEADS + N_DN_HEADS + 2 * p
        crow2 = jnp.where(lane_lo, crow[f0:f0 + 1, :], crow[f0 + 1:f0 + 2, :])
        tot = c2[c - 1:c, :] if d == 0 else c2[0:1, :]
        dec = jnp.exp(jnp.where(incl, c2 - crow2, NEG))
        ec2 = jnp.exp(c2)

        gq = _mm_nt(jnp.concatenate([k2, q2], axis=0), _blockdiag(k2, bm))
        m2 = jnp.where(strict, -(gq[:c] * beta2 * dec), 0.0)
        intra = gq[c:] * dec
        pmul = lambda a2, b2: _mm(a2, _blockdiag(b2, bm))
        md = jnp.where(blk16, m2, 0.0)
        tinv = jnp.where(ri == cj, 1.0, 0.0) + md
        pw = md
        for _ in range(3):
            pw = pmul(pw, pw)
            tinv = tinv + pmul(tinv, pw)
        for off in (jnp.where(blk32 & ~blk16, m2, 0.0), jnp.where(blk32, 0.0, m2)):
            tinv = tinv + pmul(pmul(tinv, off), tinv)
        rhs = jnp.concatenate([_blockdiag(v2 * beta2, bm), _blockdiag(k2 * beta2 * ec2, bm)], axis=1)
        uw = _mm(tinv, rhs)
        xu, xw = uw[:, :LANES], uw[:, LANES:]
        kdec = k2 * jnp.exp(tot - c2)
        qdec = q2 * ec2
        s = s_ref[d, p]
        wq = _mm(jnp.concatenate([xw, qdec], axis=0), s)
        vnew = xu - wq[:c]
        o_ref[0, :, p * LANES:(p + 1) * LANES] = wq[c:] + _mm(intra, _blockdiag(vnew, bm))
        s_ref[d, p] = s * jnp.exp(tot) + jnp.where(bm, _mm_tn(kdec, vnew), 0.0)


def _gdn_body(qkvf_ref, qkvb_ref, gbf_ref, gbb_ref, e_ref, s0_ref, of_ref, ob_ref, sfin_ref, s_ref, *, n_chunks):
    n = pl.program_id(1)

    @pl.when(n == 0)
    def _():
        s_ref[...] = s0_ref[0]

    _gdn_dir(0, qkvf_ref[0], gbf_ref[0], e_ref, s_ref, of_ref)
    _gdn_dir(1, qkvb_ref[0], gbb_ref[0], e_ref, s_ref, ob_ref)

    @pl.when(n == n_chunks - 1)
    def _():
        sfin_ref[0] = s_ref[...]


def _gdn_select_matrices():
    e = np.zeros((2, N_PAIRS, LANES, 2 * LANES), np.float32)
    for d in range(2):
        for p in range(N_PAIRS):
            for half in range(2):
                h = 2 * p + half
                e[d, p, d * 2 * N_DN_HEADS + h, half * HEAD_DIM:(half + 1) * HEAD_DIM] = 1.0
                e[d, p, d * 2 * N_DN_HEADS + N_DN_HEADS + h, LANES + half * HEAD_DIM:LANES + (half + 1) * HEAD_DIM] = 1.0
    return jnp.asarray(e, BF16)


def _gdn_call(qkvn, gb, s0):
    b, l, _ = qkvn.shape
    c = DN_CHUNK
    nc = l // c
    fwd = lambda n_: pl.BlockSpec((1, c, n_), lambda bi, n: (bi, n, 0))
    bwd = lambda n_: pl.BlockSpec((1, c, n_), lambda bi, n: (bi, nc - 1 - n, 0))
    st = pl.BlockSpec((1, 2, N_PAIRS, LANES, LANES), lambda bi, n: (bi, 0, 0, 0, 0))
    sel = _gdn_select_matrices()
    return pl.pallas_call(
        functools.partial(_gdn_body, n_chunks=nc),
        grid=(b, nc),
        in_specs=[fwd(N_QKV), bwd(N_QKV), fwd(LANES), bwd(LANES),
                  pl.BlockSpec(sel.shape, lambda bi, n: (0, 0, 0, 0)), st],
        out_specs=[fwd(D_DN), bwd(D_DN), st],
        out_shape=[jax.ShapeDtypeStruct((b, l, D_DN), F32), jax.ShapeDtypeStruct((b, l, D_DN), F32),
                   jax.ShapeDtypeStruct(s0.shape, F32)],
        scratch_shapes=[pltpu.VMEM((2, N_PAIRS, LANES, LANES), F32)],
        compiler_params=pltpu.CompilerParams(dimension_semantics=("parallel", "arbitrary"),
                                             vmem_limit_bytes=VMEM_LIMIT),
        name="gdn",
    )(qkvn, qkvn, gb, gb, sel, s0)


def _attend(q, keys, vals, mask, sink_ref, o_ref):
    lane_lo = lax.broadcasted_iota(jnp.int32, (1, LANES), 1) < HEAD_DIM
    kb, vb = keys.astype(BF16), vals.astype(BF16)
    for t in range(D_ATTN // LANES):
        qp = q[:, t * LANES:(t + 1) * LANES]
        halves = []
        for half in range(2):
            sink = sink_ref[Q_PERM[2 * t + half]]
            s = _mm_nt(jnp.where(lane_lo if half == 0 else ~lane_lo, qp, 0.0), kb)
            if mask is not None:
                s = jnp.where(mask, s, NEG)
            m = jnp.maximum(jnp.max(s, axis=-1, keepdims=True), sink)
            pr = jnp.exp(s - m)
            den = jnp.sum(pr, axis=-1, keepdims=True) + jnp.exp(sink - m)
            halves.append(jnp.dot(pr.astype(BF16), vb, preferred_element_type=F32) / den)
        o_ref[0, :, t * LANES:(t + 1) * LANES] = jnp.where(lane_lo, halves[0], halves[1])


def _attn_latent_body(sink_ref, q_ref, kvp_ref, kvc_ref, kvn_ref, kvx_ref, o_ref, *, n_blocks):
    i = pl.program_id(1)
    blk = ATTN_BLOCK
    lc = kvx_ref.shape[1]
    kv = jnp.concatenate([kvp_ref[0], kvc_ref[0], kvn_ref[0], kvx_ref[0]], axis=0)
    t = lax.broadcasted_iota(jnp.int32, (blk, 3 * blk + lc), 0)
    u = lax.broadcasted_iota(jnp.int32, (blk, 3 * blk + lc), 1)
    lo = jnp.where(i > 0, 0, blk)
    hi = jnp.where(i < n_blocks - 1, 3 * blk, 2 * blk)
    valid = (u >= 3 * blk) | ((jnp.abs(u - blk - t) <= WINDOW) & (u >= lo) & (u < hi))
    _attend(q_ref[0], kv[:, :D_KV], kv[:, D_KV:], valid, sink_ref, o_ref)


def _attn_ctx_body(sink_ref, q_ref, kvx_ref, o_ref):
    kv = kvx_ref[0]
    _attend(q_ref[0], kv[:, :D_KV], kv[:, D_KV:], None, sink_ref, o_ref)


def _attn_latent_call(qc, kvc, kvx, sink):
    b, l, _ = qc.shape
    lc = kvx.shape[1]
    nb = l // ATTN_BLOCK
    kvspec = lambda f: pl.BlockSpec((1, ATTN_BLOCK, N_KVC), lambda bi, i: (bi, f(i), 0))
    return pl.pallas_call(
        functools.partial(_attn_latent_body, n_blocks=nb),
        grid=(b, nb),
        in_specs=[pl.BlockSpec(memory_space=pltpu.SMEM),
                  pl.BlockSpec((1, ATTN_BLOCK, D_ATTN), lambda bi, i: (bi, i, 0)),
                  kvspec(lambda i: jnp.maximum(i - 1, 0)), kvspec(lambda i: i),
                  kvspec(lambda i: jnp.minimum(i + 1, nb - 1)),
                  pl.BlockSpec((1, lc, N_KVC), lambda bi, i: (bi, 0, 0))],
        out_specs=pl.BlockSpec((1, ATTN_BLOCK, D_ATTN), lambda bi, i: (bi, i, 0)),
        out_shape=jax.ShapeDtypeStruct((b, l, D_ATTN), F32),
        compiler_params=pltpu.CompilerParams(dimension_semantics=("parallel", "parallel"),
                                             vmem_limit_bytes=VMEM_LIMIT),
        name="attn_latent",
    )(sink, qc, kvc, kvc, kvc, kvx)


def _attn_ctx_call(qcx, kvx, sink):
    b, lc, _ = qcx.shape
    return pl.pallas_call(
        _attn_ctx_body,
        grid=(b, lc // ATTN_BLOCK),
        in_specs=[pl.BlockSpec(memory_space=pltpu.SMEM),
                  pl.BlockSpec((1, ATTN_BLOCK, D_ATTN), lambda bi, i: (bi, i, 0)),
                  pl.BlockSpec((1, lc, N_KVC), lambda bi, i: (bi, 0, 0))],
        out_specs=pl.BlockSpec((1, ATTN_BLOCK, D_ATTN), lambda bi, i: (bi, i, 0)),
        out_shape=jax.ShapeDtypeStruct((b, lc, D_ATTN), F32),
        compiler_params=pltpu.CompilerParams(dimension_semantics=("parallel", "parallel"),
                                             vmem_limit_bytes=VMEM_LIMIT),
        name="attn_ctx",
    )(sink, qcx, kvx)


def _outproj_body(ya_ref, of_ref, ob_ref, yc_ref, zz_ref, x_ref, mod_ref, dnw_ref, w_ref, fnw_ref, o_ref, *, final):
    acc = jnp.dot(ya_ref[0].astype(BF16), w_ref[0:D_CONV, :], preferred_element_type=F32)
    o = of_ref[0] + ob_ref[0]
    for t in range(D_DN // LANES):
        ot = o[:, t * LANES:(t + 1) * LANES]
        ms = _head_sums(ot * ot) * (1.0 / HEAD_DIM)
        yb = ot * lax.rsqrt(ms + EPS) * dnw_ref[...] * _silu(zz_ref[0, :, t * LANES:(t + 1) * LANES])
        r0 = D_CONV + t * LANES
        acc = acc + jnp.dot(yb.astype(BF16), w_ref[r0:r0 + LANES, :], preferred_element_type=F32)
    yc = yc_ref[0] * _silu(zz_ref[0, :, D_DN:])
    acc = acc + jnp.dot(yc.astype(BF16), w_ref[D_CONV + D_DN:, :], preferred_element_type=F32)
    xn = x_ref[0] + mod_ref[0, 2:3, :] * acc
    if final:
        xn = xn * lax.rsqrt(jnp.mean(xn * xn, axis=-1, keepdims=True) + EPS) * fnw_ref[...]
    o_ref[0] = xn


def _outproj_call(ya, o_f, o_b, yc, zz, x, mod3, dnw, w_p, fnw, *, final, tm):
    b, l, d = x.shape
    tok = lambda n: pl.BlockSpec((1, tm, n), lambda bi, i: (bi, i, 0))
    return pl.pallas_call(
        functools.partial(_outproj_body, final=final),
        grid=(b, l // tm),
        in_specs=[tok(D_CONV), tok(D_DN), tok(D_DN), tok(D_ATTN), tok(N_ZZ), tok(d),
                  pl.BlockSpec((1, 3, d), lambda bi, i: (bi, 0, 0)),
                  pl.BlockSpec((1, LANES), lambda bi, i: (0, 0)),
                  pl.BlockSpec(w_p.shape, lambda bi, i: (0, 0)),
                  pl.BlockSpec((1, d), lambda bi, i: (0, 0))],
        out_specs=tok(d),
        out_shape=jax.ShapeDtypeStruct((b, l, d), F32),
        compiler_params=pltpu.CompilerParams(dimension_semantics=("parallel", "parallel"),
                                             vmem_limit_bytes=VMEM_LIMIT),
        name="outproj_final" if final else "outproj",
    )(ya, o_f, o_b, yc, zz, x, mod3, dnw, w_p, fnw)


def _permute_heads(w, axis):
    parts = jnp.split(w, N_Q_HEADS, axis=axis)
    return jnp.concatenate([parts[h] for h in Q_PERM], axis=axis)


def _layout_w_in(w):
    sizes = (D_CONV, D_CONV, D_CONV, D_DN, D_DN, D_DN, D_DN, N_SC, D_ATTN, D_KV, D_KV, D_ATTN)
    (a_val, a_gate, z_a, q_b, k_b, v_b, z_b, sc, q_c, k_c, v_c, z_c) = jnp.split(w, np.cumsum(sizes)[:-1].tolist(), axis=1)
    sc_pad = jnp.pad(sc, ((0, 0), (0, LANES - N_SC)))
    cols = [a_val, a_gate, z_a, q_b, k_b, v_b, sc_pad, _permute_heads(q_c, 1), k_c, v_c, z_b, _permute_heads(z_c, 1)]
    return jnp.concatenate(cols, axis=1).astype(BF16)


def _layout_w_out(w):
    return jnp.concatenate([w[:D_CONV + D_DN], _permute_heads(w[D_CONV + D_DN:], 0)], axis=0).astype(BF16)


def _lane_params(a_log, dt_bias):
    z = jnp.zeros((2, N_DN_HEADS), F32)
    one = jnp.ones((2, N_DN_HEADS), F32)
    rows = [jnp.stack([z, a_log], axis=1), jnp.stack([z, dt_bias], axis=1), jnp.stack([z, one], axis=1)]
    rows = jnp.stack([r.reshape(N_SC) for r in rows], axis=0)
    return jnp.pad(rows, ((0, 8 - rows.shape[0]), (0, LANES - N_SC)))


def _rope_tables(n_tok):
    quarter = HEAD_DIM // 4
    inv = ROPE_BASE ** (-jnp.arange(quarter, dtype=F32) / quarter)
    pos = jnp.arange(n_tok, dtype=jnp.int32)
    rows, cols = (pos // GRID_W).astype(F32), (pos % GRID_W).astype(F32)
    ang = jnp.concatenate([rows[:, None] * inv, rows[:, None] * inv, cols[:, None] * inv, cols[:, None] * inv], axis=1)
    sign = jnp.tile(jnp.concatenate([-jnp.ones(quarter, F32), jnp.ones(quarter, F32)]), 2)
    cos, sin = jnp.cos(ang), jnp.sin(ang) * sign
    return jnp.tile(cos, (1, LANES // HEAD_DIM)), jnp.tile(sin, (1, LANES // HEAD_DIM))


def _pick_tile(n, pref):
    t = min(pref, n)
    while n % t:
        t //= 2
    return t


def kernel(x, c, ctx, c_ctx, norm_w, ada_w, ada_b, w_in, conv_a_w, conv_a_b, ln_a_w, ln_a_b, qkv_conv_w, a_log,
           dt_bias, dn_norm_w, sink, w_out, final_norm_w):
    b, l, d = x.shape
    lc = ctx.shape[1]
    depth = w_in.shape[0]
    assert l % ATTN_BLOCK == 0 and lc % ATTN_BLOCK == 0 and l % GRID_W == 0

    rows = -(-(b + 1) // 8) * 8
    cond = jnp.zeros((rows, d), F32).at[:b].set(c).at[b].set(c_ctx)
    mod = _ada_call(cond, ada_w, ada_b).reshape(depth, rows, 3, d)
    cos_t, sin_t = _rope_tables(l)
    s_zero = jnp.zeros((b, 2, N_PAIRS, LANES, LANES), F32)
    tm = _pick_tile(l, 256)
    tmc = _pick_tile(lc, 256)

    xc = ctx
    for li in range(depth):
        last = li == depth - 1
        mod_x = mod[li, :b]
        mod_c = jnp.broadcast_to(mod[li, b][None], (b, 3, d))
        w_p = _layout_w_in(w_in[li])
        wo_p = _layout_w_out(w_out[li])
        lanep = _lane_params(a_log[li], dt_bias[li])
        nw = norm_w[li].reshape(1, d)
        dnw = jnp.tile(dn_norm_w[li], LANES // HEAD_DIM).reshape(1, LANES)
        fnw = final_norm_w.reshape(1, d)
        mix = (conv_a_w[li], conv_a_b[li], ln_a_w[li], ln_a_b[li], qkv_conv_w[li])

        pa, pqkv, gb, qc, kvc, zz = _inproj_call(x, mod_x, nw, w_p, lanep, cos_t, sin_t, rope=True, tm=tm)
        pa_x, pqkv_x, gb_x, qc_x, kvc_x, zz_x = _inproj_call(xc, mod_c, nw, w_p, lanep, cos_t[:lc], sin_t[:lc],
                                                             rope=False, tm=tmc)
        ya, qkvn = _mixprep_call(pa, pqkv, *mix, tile=tm)
        ya_x, qkvn_x = _mixprep_call(pa_x, pqkv_x, *mix, tile=tmc)
        of_x, ob_x, s_ctx = _gdn_call(qkvn_x, gb_x, s_zero)
        o_f, o_b, _ = _gdn_call(qkvn, gb, s_ctx)
        yc = _attn_latent_call(qc, kvc, kvc_x, sink[li])
        x = _outproj_call(ya, o_f, o_b, yc, zz, x, mod_x, dnw, wo_p, fnw, final=last, tm=tm)
        if not last:
            yc_x = _attn_ctx_call(qc_x, kvc_x, sink[li])
            xc = _outproj_call(ya_x, of_x, ob_x, yc_x, zz_x, xc, mod_c, dnw, wo_p, fnw, final=False, tm=tmc)
    return x
```

```python
import functools

import jax
import jax.numpy as jnp
import numpy as np
from jax import lax
from jax.experimental import pallas as pl
from jax.experimental.pallas import tpu as pltpu

F32 = jnp.float32
BF16 = jnp.bfloat16

HEAD_DIM = 64
LANES = 128
D_CONV = 256
CONV_WIDTH = 31
N_DN_HEADS = 6
D_DN = N_DN_HEADS * HEAD_DIM
SHORT_CONV = 5
DN_CHUNK = 64
N_Q_HEADS = 6
N_KV_HEADS = 2
D_ATTN = N_Q_HEADS * HEAD_DIM
D_KV = N_KV_HEADS * HEAD_DIM
WINDOW = 128
ATTN_BLOCK = 128
ROPE_BASE = 10000.0
GRID_W = 64
EPS = 1e-6
N_PAIRS = N_DN_HEADS // 2
HALO = 16
NEG = -1e30
VMEM_LIMIT = 48 * 1024 * 1024
GDN_BATCH_ROWS = 4

N_A = 3 * D_CONV
N_QKV = 3 * D_DN
N_SC = 4 * N_DN_HEADS
N_KVC = 2 * D_KV
N_ZZ = D_DN + D_ATTN
OFF_QKV = N_A
OFF_SC = OFF_QKV + N_QKV
OFF_QC = OFF_SC + LANES
OFF_KVC = OFF_QC + D_ATTN
OFF_ZZ = OFF_KVC + N_KVC
N_IN_PAD = OFF_ZZ + N_ZZ
Q_PERM = (0, 3, 1, 4, 2, 5)


def _silu(x):
    return x * jax.nn.sigmoid(x)


def _mm(a, b):
    return jnp.dot(a.astype(BF16), b.astype(BF16), preferred_element_type=F32)


def _mm_nt(a, b):
    return lax.dot_general(a.astype(BF16), b.astype(BF16), (((1,), (1,)), ((), ())), preferred_element_type=F32)


def _mm_tn(a, b):
    return lax.dot_general(a.astype(BF16), b.astype(BF16), (((0,), (0,)), ((), ())), preferred_element_type=F32)


def _split2(x):
    hi = x.astype(BF16)
    lo = (x - hi.astype(F32)).astype(BF16)
    return hi, lo


def _mm_x2(a, b01):
    hi, lo = _split2(a)
    return (jnp.dot(hi, b01, preferred_element_type=F32) + jnp.dot(lo, b01, preferred_element_type=F32))


def _head_sums(x):
    r = lax.broadcasted_iota(jnp.int32, (LANES, LANES), 0) // HEAD_DIM
    c = lax.broadcasted_iota(jnp.int32, (LANES, LANES), 1) // HEAD_DIM
    ones_bd = jnp.where(r == c, 1.0, 0.0).astype(BF16)
    return _mm_x2(x, ones_bd)


def _ada_body(cond_ref, w_ref, b_ref, o_ref):
    a = _silu(cond_ref[...])
    o_ref[0] = jnp.dot(a, w_ref[0], preferred_element_type=F32, precision=lax.Precision.HIGHEST) + b_ref[0]


def _ada_call(cond, ada_w, ada_b):
    depth, d, d3 = ada_w.shape
    rows = cond.shape[0]
    tn = 1024
    return pl.pallas_call(
        _ada_body,
        grid=(depth, d3 // tn),
        in_specs=[pl.BlockSpec((rows, d), lambda l, j: (0, 0)),
                  pl.BlockSpec((1, d, tn), lambda l, j: (l, 0, j)),
                  pl.BlockSpec((1, 1, tn), lambda l, j: (l, 0, j))],
        out_specs=pl.BlockSpec((1, rows, tn), lambda l, j: (l, 0, j)),
        out_shape=jax.ShapeDtypeStruct((depth, rows, d3), F32),
        compiler_params=pltpu.CompilerParams(vmem_limit_bytes=VMEM_LIMIT),
        name="ada",
    )(cond, ada_w, ada_b.reshape(depth, 1, d3))


def _inproj_body(x_ref, mod_ref, nw_ref, w_ref, lanep_ref, cos_ref, sin_ref,
                 pa_ref, pqkv_ref, gb_ref, qc_ref, kvc_ref, zz_ref, *, rope):
    x = x_ref[0]
    y = x * lax.rsqrt(jnp.mean(x * x, axis=-1, keepdims=True) + EPS) * nw_ref[...]
    h = (y * (1.0 + mod_ref[0, 1:2, :]) + mod_ref[0, 0:1, :]).astype(BF16)

    def proj(off, n):
        return jnp.dot(h, w_ref[:, off:off + n], preferred_element_type=F32)

    pa_ref[0] = proj(0, N_A)
    pqkv_ref[0] = proj(OFF_QKV, N_QKV)
    zz_ref[0] = proj(OFF_ZZ, N_ZZ)

    s = proj(OFF_SC, LANES)
    a_log, dt_b, is_decay = lanep_ref[0:1, :], lanep_ref[1:2, :], lanep_ref[2:3, :] > 0.5
    t = s + dt_b
    softplus = jnp.maximum(t, 0.0) + jnp.log(1.0 + jnp.exp(-jnp.abs(t)))
    gb_ref[0] = jnp.where(is_decay, -jnp.exp(a_log) * softplus, jax.nn.sigmoid(s))

    qk = proj(OFF_QC, D_ATTN + D_KV)
    v = proj(OFF_KVC + D_KV, D_KV)
    lane = lax.broadcasted_iota(jnp.int32, (1, LANES), 1)
    first = (lane % (HEAD_DIM // 2)) < (HEAD_DIM // 4)
    for t_ in range((D_ATTN + D_KV) // LANES):
        tile = qk[:, t_ * LANES:(t_ + 1) * LANES]
        if rope:
            partner = jnp.where(first, pltpu.roll(tile, LANES - HEAD_DIM // 4, 1), pltpu.roll(tile, HEAD_DIM // 4, 1))
            tile = tile * cos_ref[...] + partner * sin_ref[...]
        if t_ < D_ATTN // LANES:
            qc_ref[0, :, t_ * LANES:(t_ + 1) * LANES] = tile * (HEAD_DIM ** -0.5)
        else:
            kvc_ref[0, :, 0:D_KV] = tile
    kvc_ref[0, :, D_KV:] = v


def _inproj_call(x, mod3, norm_w, w_p, lanep, cos_t, sin_t, *, rope, tm):
    b, l, d = x.shape
    outs = (N_A, N_QKV, LANES, D_ATTN, N_KVC, N_ZZ)
    tok = lambda n: pl.BlockSpec((1, tm, n), lambda bi, i: (bi, i, 0))
    return pl.pallas_call(
        functools.partial(_inproj_body, rope=rope),
        grid=(b, l // tm),
        in_specs=[tok(d),
                  pl.BlockSpec((1, 3, d), lambda bi, i: (bi, 0, 0)),
                  pl.BlockSpec((1, d), lambda bi, i: (0, 0)),
                  pl.BlockSpec((d, N_IN_PAD), lambda bi, i: (0, 0)),
                  pl.BlockSpec((8, LANES), lambda bi, i: (0, 0)),
                  pl.BlockSpec((tm, LANES), lambda bi, i: (i, 0)),
                  pl.BlockSpec((tm, LANES), lambda bi, i: (i, 0))],
        out_specs=[tok(n) for n in outs],
        out_shape=[jax.ShapeDtypeStruct((b, l, n), F32) for n in outs],
        compiler_params=pltpu.CompilerParams(dimension_semantics=("parallel", "parallel"),
                                             vmem_limit_bytes=VMEM_LIMIT),
        name="inproj_rope" if rope else "inproj",
    )(x, mod3, norm_w, w_p, lanep, cos_t, sin_t)


def _fill_ext(ext_ref, main, prev, nxt, i, n_tiles, tile):
    ext_ref[0:HALO, :] = jnp.where(i > 0, prev, 0.0)
    ext_ref[HALO:HALO + tile, :] = main
    ext_ref[HALO + tile:, :] = jnp.where(i < n_tiles - 1, nxt, 0.0)


def _dwconv(ext_ref, w_ref, width, rows, r0):
    pad = (width - 1) // 2
    acc = None
    for k in range(width):
        term = ext_ref[r0 + HALO - pad + k:r0 + HALO - pad + k + rows, :] * w_ref[k:k + 1, :]
        acc = term if acc is None else acc + term
    return acc


def _mixprep_body(pa_ref, pap_ref, pan_ref, q_ref, qp_ref, qn_ref, caw_ref, cab_ref, lnw_ref, lnb_ref, qcw_ref,
                  ya_ref, qkvn_ref, exta_ref, extq_ref, *, tile, n_tiles):
    i = pl.program_id(1)
    glu = lambda t: t[:, 0:D_CONV] * jax.nn.sigmoid(t[:, D_CONV:2 * D_CONV])
    _fill_ext(exta_ref, glu(pa_ref[0]), glu(pap_ref[0]), glu(pan_ref[0]), i, n_tiles, tile)
    _fill_ext(extq_ref, q_ref[0], qp_ref[0], qn_ref[0], i, n_tiles, tile)
    sub = 32
    for r0 in range(0, tile, sub):
        u = _dwconv(exta_ref, caw_ref, CONV_WIDTH, sub, r0) + cab_ref[...]
        mu = jnp.mean(u, axis=-1, keepdims=True)
        var = jnp.mean(jnp.square(u - mu), axis=-1, keepdims=True)
        u = (u - mu) * lax.rsqrt(var + EPS) * lnw_ref[...] + lnb_ref[...]
        ya_ref[0, r0:r0 + sub, :] = _silu(u) * _silu(pa_ref[0, r0:r0 + sub, 2 * D_CONV:])
    for r0 in range(0, tile, sub):
        t = _silu(_dwconv(extq_ref, qcw_ref, SHORT_CONV, sub, r0))
        for c0 in range(0, N_QKV, LANES):
            tl = t[:, c0:c0 + LANES]
            if c0 < 2 * D_DN:
                tl = tl * lax.rsqrt(_head_sums(tl * tl) + EPS)
                if c0 < D_DN:
                    tl = tl * (HEAD_DIM ** -0.5)
            qkvn_ref[0, r0:r0 + sub, c0:c0 + LANES] = tl


def _mixprep_call(pa, pqkv, conv_a_w, conv_a_b, ln_w, ln_b, qkv_conv_w, *, tile):
    b, l, _ = pa.shape
    n_tiles = l // tile
    hpt = tile // HALO
    n_h = l // HALO
    main = lambda n: pl.BlockSpec((1, tile, n), lambda bi, i: (bi, i, 0))
    prev = lambda n: pl.BlockSpec((1, HALO, n), lambda bi, i: (bi, jnp.maximum(i * hpt - 1, 0), 0))
    nxt = lambda n: pl.BlockSpec((1, HALO, n), lambda bi, i: (bi, jnp.minimum((i + 1) * hpt, n_h - 1), 0))
    full = lambda a: pl.BlockSpec(a.shape, lambda bi, i: (0, 0))
    consts = (conv_a_w, conv_a_b.reshape(1, -1), ln_w.reshape(1, -1), ln_b.reshape(1, -1), qkv_conv_w)
    return pl.pallas_call(
        functools.partial(_mixprep_body, tile=tile, n_tiles=n_tiles),
        grid=(b, n_tiles),
        in_specs=[main(N_A), prev(N_A), nxt(N_A), main(N_QKV), prev(N_QKV), nxt(N_QKV)] + [full(a) for a in consts],
        out_specs=[main(D_CONV), main(N_QKV)],
        out_shape=[jax.ShapeDtypeStruct((b, l, D_CONV), F32), jax.ShapeDtypeStruct((b, l, N_QKV), F32)],
        scratch_shapes=[pltpu.VMEM((tile + 2 * HALO, D_CONV), F32), pltpu.VMEM((tile + 2 * HALO, N_QKV), F32)],
        compiler_params=pltpu.CompilerParams(dimension_semantics=("parallel", "parallel"),
                                             vmem_limit_bytes=VMEM_LIMIT),
        name="mixprep",
    )(pa, pa, pa, pqkv, pqkv, pqkv, *consts)


def _blockdiag(x2, bm):
    return jnp.where(bm, jnp.concatenate([x2, x2], axis=0), 0.0)


def _gdn_chunk(dirs, s_ref):
    c = DN_CHUNK
    ri = lax.broadcasted_iota(jnp.int32, (c, LANES), 0)
    cj = lax.broadcasted_iota(jnp.int32, (c, LANES), 1) % HEAD_DIM
    blk16 = (ri // 16) == (cj // 16)
    blk32 = (ri // 32) == (cj // 32)
    eye = jnp.where(ri == cj, 1.0, 0.0)
    bm = (lax.broadcasted_iota(jnp.int32, (LANES, LANES), 0) // HEAD_DIM
          == lax.broadcasted_iota(jnp.int32, (LANES, LANES), 1) // HEAD_DIM)
    ti = lax.broadcasted_iota(jnp.int32, (c, c), 0)
    tt = lax.broadcasted_iota(jnp.int32, (c, c), 1)
    feat = lax.broadcasted_iota(jnp.int32, (1, LANES), 1)
    is_decay = (feat % (2 * N_DN_HEADS)) >= N_DN_HEADS
    lane_lo = feat < HEAD_DIM
    tn = (((0,), (0,)), ((), ()))

    per_dir = {}
    for bi, d, qkv, gb, o_ref in dirs:
        tri = jnp.where((tt <= ti) if d == 0 else (tt >= ti), 1.0, 0.0).astype(BF16)
        tri_t2 = jnp.where((ri <= cj) if d == 0 else (ri >= cj), 1.0, 0.0).astype(BF16)
        g_hi, g_lo = _split2(gb)
        csum = jnp.dot(tri, g_hi, preferred_element_type=F32) + jnp.dot(tri, g_lo, preferred_element_type=F32)
        crow = (lax.dot_general(g_hi, tri_t2, tn, preferred_element_type=F32)
                + lax.dot_general(g_lo, tri_t2, tn, preferred_element_type=F32))
        src = jnp.where(is_decay, csum, gb)
        incl = (ri >= cj) if d == 0 else (ri <= cj)
        strict = (ri > cj) if d == 0 else (ri < cj)
        per_dir[bi, d] = (qkv, o_ref, crow, src, incl, strict)

    chains = [(bi, d, p) for bi, d, _, _, _ in dirs for p in range(N_PAIRS)]
    each = lambda f, *cols: [f(*args) for args in zip(*cols)]

    def load(bi, d, p):
        qkv = per_dir[bi, d][0]
        return tuple(qkv[:, part * D_DN + p * LANES:part * D_DN + (p + 1) * LANES] for part in range(3))

    q2, k2, v2 = zip(*[load(*ch) for ch in chains])

    def spread(bi, d, p, kind):
        src = per_dir[bi, d][3]
        f = d * 2 * N_DN_HEADS + kind * N_DN_HEADS + 2 * p
        return jnp.where(lane_lo, src[:, f:f + 1], src[:, f + 1:f + 2])

    beta2 = [spread(*ch, 0) for ch in chains]
    c2 = [spread(*ch, 1) for ch in chains]

    def decay(ch, c2_):
        bi, d, p = ch
        crow, incl = per_dir[bi, d][2], per_dir[bi, d][4]
        f0 = d * 2 * N_DN_HEADS + N_DN_HEADS + 2 * p
        crow2 = jnp.where(lane_lo, crow[f0:f0 + 1, :], crow[f0 + 1:f0 + 2, :])
        return jnp.exp(jnp.where(incl, c2_ - crow2, NEG))

    dec = each(decay, chains, c2)
    tot = [c2_[c - 1:c, :] if d == 0 else c2_[0:1, :] for (_, d, _), c2_ in zip(chains, c2)]
    ec2 = [jnp.exp(t) for t in c2]
    gq = each(lambda k, q: _mm_nt(jnp.concatenate([k, q], axis=0), _blockdiag(k, bm)), k2, q2)
    m2 = [jnp.where(per_dir[bi, d][5], -(g[:c] * b_ * dc), 0.0)
          for (bi, d, _), g, b_, dc in zip(chains, gq, beta2, dec)]
    intra = [g[c:] * dc for g, dc in zip(gq, dec)]

    pmul = lambda a2, b2: _mm(a2, _blockdiag(b2, bm))
    md = [jnp.where(blk16, m, 0.0) for m in m2]
    tinv = [eye + m for m in md]
    pw = each(pmul, md, md)
    for _ in range(2):
        both = each(lambda t, x: pmul(jnp.concatenate([t, x], axis=0), x), tinv, pw)
        tinv = each(lambda t, y: t + y[:c], tinv, both)
        pw = [y[c:] for y in both]
    tinv = each(lambda t, x: t + pmul(t, x), tinv, pw)
    for off in ([jnp.where(blk32 & ~blk16, m, 0.0) for m in m2], [jnp.where(blk32, 0.0, m) for m in m2]):
        half = each(pmul, tinv, off)
        tinv = each(lambda t, h: t + pmul(h, t), tinv, half)
    rhs = each(lambda v, k, b_, e: jnp.concatenate([_blockdiag(v * b_, bm), _blockdiag(k * b_ * e, bm)], axis=1),
               v2, k2, beta2, ec2)
    uw = each(_mm, tinv, rhs)
    kdec = each(lambda k, t, c2_: k * jnp.exp(t - c2_), k2, tot, c2)
    s_old = [s_ref[bi, d, p] for bi, d, p in chains]
    wq = each(lambda u, q, e, s: _mm(jnp.concatenate([u[:, LANES:], q * e], axis=0), s), uw, q2, ec2, s_old)
    vnew = each(lambda u, w: u[:, :LANES] - w[:c], uw, wq)
    o2 = each(lambda w, a, v: w[c:] + _mm(a, _blockdiag(v, bm)), wq, intra, vnew)
    s_new = each(lambda s, t, k, v: s * jnp.exp(t) + jnp.where(bm, _mm_tn(k, v), 0.0), s_old, tot, kdec, vnew)
    for (bi, d, p), o, s in zip(chains, o2, s_new):
        per_dir[bi, d][1][bi, :, p * LANES:(p + 1) * LANES] = o
        s_ref[bi, d, p] = s


def _gdn_body(qkvf_ref, qkvb_ref, gbf_ref, gbb_ref, s0_ref, of_ref, ob_ref, sfin_ref, s_ref, *, n_chunks, bb):
    n = pl.program_id(1)

    @pl.when(n == 0)
    def _():
        s_ref[...] = s0_ref[...]

    dirs = []
    for bi in range(bb):
        dirs += [(bi, 0, qkvf_ref[bi], gbf_ref[bi], of_ref), (bi, 1, qkvb_ref[bi], gbb_ref[bi], ob_ref)]
    _gdn_chunk(dirs, s_ref)

    @pl.when(n == n_chunks - 1)
    def _():
        sfin_ref[...] = s_ref[...]


def _gdn_call(qkvn, gb, s0):
    b, l, _ = qkvn.shape
    c = DN_CHUNK
    nc = l // c
    bb = _pick_tile(b, GDN_BATCH_ROWS)
    fwd = lambda n_: pl.BlockSpec((bb, c, n_), lambda bi, n: (bi, n, 0))
    bwd = lambda n_: pl.BlockSpec((bb, c, n_), lambda bi, n: (bi, nc - 1 - n, 0))
    st = pl.BlockSpec((bb, 2, N_PAIRS, LANES, LANES), lambda bi, n: (bi, 0, 0, 0, 0))
    return pl.pallas_call(
        functools.partial(_gdn_body, n_chunks=nc, bb=bb),
        grid=(b // bb, nc),
        in_specs=[fwd(N_QKV), bwd(N_QKV), fwd(LANES), bwd(LANES), st],
        out_specs=[fwd(D_DN), bwd(D_DN), st],
        out_shape=[jax.ShapeDtypeStruct((b, l, D_DN), F32), jax.ShapeDtypeStruct((b, l, D_DN), F32),
                   jax.ShapeDtypeStruct(s0.shape, F32)],
        scratch_shapes=[pltpu.VMEM((bb, 2, N_PAIRS, LANES, LANES), F32)],
        compiler_params=pltpu.CompilerParams(dimension_semantics=("parallel", "arbitrary"),
                                             vmem_limit_bytes=VMEM_LIMIT),
        name="gdn",
    )(qkvn, qkvn, gb, gb, s0)


def _attend(q, keys, vals, mask, sink_ref, o_ref):
    lane_lo = lax.broadcasted_iota(jnp.int32, (1, LANES), 1) < HEAD_DIM
    kb, vb = keys.astype(BF16), vals.astype(BF16)
    for t in range(D_ATTN // LANES):
        qp = q[:, t * LANES:(t + 1) * LANES]
        halves = []
        for half in range(2):
            sink = sink_ref[Q_PERM[2 * t + half]]
            s = _mm_nt(jnp.where(lane_lo if half == 0 else ~lane_lo, qp, 0.0), kb)
            if mask is not None:
                s = jnp.where(mask, s, NEG)
            m = jnp.maximum(jnp.max(s, axis=-1, keepdims=True), sink)
            pr = jnp.exp(s - m)
            den = jnp.sum(pr, axis=-1, keepdims=True) + jnp.exp(sink - m)
            halves.append(jnp.dot(pr.astype(BF16), vb, preferred_element_type=F32) / den)
        o_ref[0, :, t * LANES:(t + 1) * LANES] = jnp.where(lane_lo, halves[0], halves[1])


def _attn_latent_body(sink_ref, q_ref, kvp_ref, kvc_ref, kvn_ref, kvx_ref, o_ref, *, n_blocks):
    i = pl.program_id(1)
    blk = ATTN_BLOCK
    lc = kvx_ref.shape[1]
    kv = jnp.concatenate([kvp_ref[0], kvc_ref[0], kvn_ref[0], kvx_ref[0]], axis=0)
    t = lax.broadcasted_iota(jnp.int32, (blk, 3 * blk + lc), 0)
    u = lax.broadcasted_iota(jnp.int32, (blk, 3 * blk + lc), 1)
    lo = jnp.where(i > 0, 0, blk)
    hi = jnp.where(i < n_blocks - 1, 3 * blk, 2 * blk)
    valid = (u >= 3 * blk) | ((jnp.abs(u - blk - t) <= WINDOW) & (u >= lo) & (u < hi))
    _attend(q_ref[0], kv[:, :D_KV], kv[:, D_KV:], valid, sink_ref, o_ref)


def _attn_ctx_body(sink_ref, q_ref, kvx_ref, o_ref):
    kv = kvx_ref[0]
    _attend(q_ref[0], kv[:, :D_KV], kv[:, D_KV:], None, sink_ref, o_ref)


def _attn_latent_call(qc, kvc, kvx, sink):
    b, l, _ = qc.shape
    lc = kvx.shape[1]
    nb = l // ATTN_BLOCK
    kvspec = lambda f: pl.BlockSpec((1, ATTN_BLOCK, N_KVC), lambda bi, i: (bi, f(i), 0))
    return pl.pallas_call(
        functools.partial(_attn_latent_body, n_blocks=nb),
        grid=(b, nb),
        in_specs=[pl.BlockSpec(memory_space=pltpu.SMEM),
                  pl.BlockSpec((1, ATTN_BLOCK, D_ATTN), lambda bi, i: (bi, i, 0)),
                  kvspec(lambda i: jnp.maximum(i - 1, 0)), kvspec(lambda i: i),
                  kvspec(lambda i: jnp.minimum(i + 1, nb - 1)),
                  pl.BlockSpec((1, lc, N_KVC), lambda bi, i: (bi, 0, 0))],
        out_specs=pl.BlockSpec((1, ATTN_BLOCK, D_ATTN), lambda bi, i: (bi, i, 0)),
        out_shape=jax.ShapeDtypeStruct((b, l, D_ATTN), F32),
        compiler_params=pltpu.CompilerParams(dimension_semantics=("parallel", "parallel"),
                                             vmem_limit_bytes=VMEM_LIMIT),
        name="attn_latent",
    )(sink, qc, kvc, kvc, kvc, kvx)


def _attn_ctx_call(qcx, kvx, sink):
    b, lc, _ = qcx.shape
    return pl.pallas_call(
        _attn_ctx_body,
        grid=(b, lc // ATTN_BLOCK),
        in_specs=[pl.BlockSpec(memory_space=pltpu.SMEM),
                  pl.BlockSpec((1, ATTN_BLOCK, D_ATTN), lambda bi, i: (bi, i, 0)),
                  pl.BlockSpec((1, lc, N_KVC), lambda bi, i: (bi, 0, 0))],
        out_specs=pl.BlockSpec((1, ATTN_BLOCK, D_ATTN), lambda bi, i: (bi, i, 0)),
        out_shape=jax.ShapeDtypeStruct((b, lc, D_ATTN), F32),
        compiler_params=pltpu.CompilerParams(dimension_semantics=("parallel", "parallel"),
                                             vmem_limit_bytes=VMEM_LIMIT),
        name="attn_ctx",
    )(sink, qcx, kvx)


def _outproj_body(ya_ref, of_ref, ob_ref, yc_ref, zz_ref, x_ref, mod_ref, dnw_ref, w_ref, fnw_ref, o_ref, *, final):
    acc = jnp.dot(ya_ref[0].astype(BF16), w_ref[0:D_CONV, :], preferred_element_type=F32)
    o = of_ref[0] + ob_ref[0]
    for t in range(D_DN // LANES):
        ot = o[:, t * LANES:(t + 1) * LANES]
        ms = _head_sums(ot * ot) * (1.0 / HEAD_DIM)
        yb = ot * lax.rsqrt(ms + EPS) * dnw_ref[...] * _silu(zz_ref[0, :, t * LANES:(t + 1) * LANES])
        r0 = D_CONV + t * LANES
        acc = acc + jnp.dot(yb.astype(BF16), w_ref[r0:r0 + LANES, :], preferred_element_type=F32)
    yc = yc_ref[0] * _silu(zz_ref[0, :, D_DN:])
    acc = acc + jnp.dot(yc.astype(BF16), w_ref[D_CONV + D_DN:, :], preferred_element_type=F32)
    xn = x_ref[0] + mod_ref[0, 2:3, :] * acc
    if final:
        xn = xn * lax.rsqrt(jnp.mean(xn * xn, axis=-1, keepdims=True) + EPS) * fnw_ref[...]
    o_ref[0] = xn


def _outproj_call(ya, o_f, o_b, yc, zz, x, mod3, dnw, w_p, fnw, *, final, tm):
    b, l, d = x.shape
    tok = lambda n: pl.BlockSpec((1, tm, n), lambda bi, i: (bi, i, 0))
    return pl.pallas_call(
        functools.partial(_outproj_body, final=final),
        grid=(b, l // tm),
        in_specs=[tok(D_CONV), tok(D_DN), tok(D_DN), tok(D_ATTN), tok(N_ZZ), tok(d),
                  pl.BlockSpec((1, 3, d), lambda bi, i: (bi, 0, 0)),
                  pl.BlockSpec((1, LANES), lambda bi, i: (0, 0)),
                  pl.BlockSpec(w_p.shape, lambda bi, i: (0, 0)),
                  pl.BlockSpec((1, d), lambda bi, i: (0, 0))],
        out_specs=tok(d),
        out_shape=jax.ShapeDtypeStruct((b, l, d), F32),
        compiler_params=pltpu.CompilerParams(dimension_semantics=("parallel", "parallel"),
                                             vmem_limit_bytes=VMEM_LIMIT),
        name="outproj_final" if final else "outproj",
    )(ya, o_f, o_b, yc, zz, x, mod3, dnw, w_p, fnw)


def _permute_heads(w, axis):
    parts = jnp.split(w, N_Q_HEADS, axis=axis)
    return jnp.concatenate([parts[h] for h in Q_PERM], axis=axis)


def _layout_w_in(w):
    sizes = (D_CONV, D_CONV, D_CONV, D_DN, D_DN, D_DN, D_DN, N_SC, D_ATTN, D_KV, D_KV, D_ATTN)
    (a_val, a_gate, z_a, q_b, k_b, v_b, z_b, sc, q_c, k_c, v_c, z_c) = jnp.split(w, np.cumsum(sizes)[:-1].tolist(), axis=1)
    sc_pad = jnp.pad(sc, ((0, 0), (0, LANES - N_SC)))
    cols = [a_val, a_gate, z_a, q_b, k_b, v_b, sc_pad, _permute_heads(q_c, 1), k_c, v_c, z_b, _permute_heads(z_c, 1)]
    return jnp.concatenate(cols, axis=1).astype(BF16)


def _layout_w_out(w):
    return jnp.concatenate([w[:D_CONV + D_DN], _permute_heads(w[D_CONV + D_DN:], 0)], axis=0).astype(BF16)


def _lane_params(a_log, dt_bias):
    z = jnp.zeros((2, N_DN_HEADS), F32)
    one = jnp.ones((2, N_DN_HEADS), F32)
    rows = [jnp.stack([z, a_log], axis=1), jnp.stack([z, dt_bias], axis=1), jnp.stack([z, one], axis=1)]
    rows = jnp.stack([r.reshape(N_SC) for r in rows], axis=0)
    return jnp.pad(rows, ((0, 8 - rows.shape[0]), (0, LANES - N_SC)))


def _rope_tables(n_tok):
    quarter = HEAD_DIM // 4
    inv = ROPE_BASE ** (-jnp.arange(quarter, dtype=F32) / quarter)
    pos = jnp.arange(n_tok, dtype=jnp.int32)
    rows, cols = (pos // GRID_W).astype(F32), (pos % GRID_W).astype(F32)
    ang = jnp.concatenate([rows[:, None] * inv, rows[:, None] * inv, cols[:, None] * inv, cols[:, None] * inv], axis=1)
    sign = jnp.tile(jnp.concatenate([-jnp.ones(quarter, F32), jnp.ones(quarter, F32)]), 2)
    cos, sin = jnp.cos(ang), jnp.sin(ang) * sign
    return jnp.tile(cos, (1, LANES // HEAD_DIM)), jnp.tile(sin, (1, LANES // HEAD_DIM))


def _pick_tile(n, pref):
    t = min(pref, n)
    while n % t:
        t //= 2
    return t


def kernel(x, c, ctx, c_ctx, norm_w, ada_w, ada_b, w_in, conv_a_w, conv_a_b, ln_a_w, ln_a_b, qkv_conv_w, a_log,
           dt_bias, dn_norm_w, sink, w_out, final_norm_w):
    b, l, d = x.shape
    lc = ctx.shape[1]
    depth = w_in.shape[0]
    assert l % ATTN_BLOCK == 0 and lc % ATTN_BLOCK == 0 and l % GRID_W == 0

    rows = -(-(b + 1) // 8) * 8
    cond = jnp.zeros((rows, d), F32).at[:b].set(c).at[b].set(c_ctx)
    mod = _ada_call(cond, ada_w, ada_b).reshape(depth, rows, 3, d)
    cos_t, sin_t = _rope_tables(l)
    s_zero = jnp.zeros((b, 2, N_PAIRS, LANES, LANES), F32)
    tm = _pick_tile(l, 256)
    tmc = _pick_tile(lc, 256)

    xc = ctx
    for li in range(depth):
        last = li == depth - 1
        mod_x = mod[li, :b]
        mod_c = jnp.broadcast_to(mod[li, b][None], (b, 3, d))
        w_p = _layout_w_in(w_in[li])
        wo_p = _layout_w_out(w_out[li])
        lanep = _lane_params(a_log[li], dt_bias[li])
        nw = norm_w[li].reshape(1, d)
        dnw = jnp.tile(dn_norm_w[li], LANES // HEAD_DIM).reshape(1, LANES)
        fnw = final_norm_w.reshape(1, d)
        mix = (conv_a_w[li], conv_a_b[li], ln_a_w[li], ln_a_b[li], qkv_conv_w[li])

        pa, pqkv, gb, qc, kvc, zz = _inproj_call(x, mod_x, nw, w_p, lanep, cos_t, sin_t, rope=True, tm=tm)
        pa_x, pqkv_x, gb_x, qc_x, kvc_x, zz_x = _inproj_call(xc, mod_c, nw, w_p, lanep, cos_t[:lc], sin_t[:lc],
                                                             rope=False, tm=tmc)
        ya, qkvn = _mixprep_call(pa, pqkv, *mix, tile=tm)
        ya_x, qkvn_x = _mixprep_call(pa_x, pqkv_x, *mix, tile=tmc)
        of_x, ob_x, s_ctx = _gdn_call(qkvn_x, gb_x, s_zero)
        o_f, o_b, _ = _gdn_call(qkvn, gb, s_ctx)
        yc = _attn_latent_call(qc, kvc, kvc_x, sink[li])
        x = _outproj_call(ya, o_f, o_b, yc, zz, x, mod_x, dnw, wo_p, fnw, final=last, tm=tm)
        if not last:
            yc_x = _attn_ctx_call(qc_x, kvc_x, sink[li])
            xc = _outproj_call(ya_x, of_x, ob_x, yc_x, zz_x, xc, mod_c, dnw, wo_p, fnw, final=False, tm=tmc)
    return x
```

```python
import functools

import jax
import jax.numpy as jnp
import numpy as np
from jax import lax
from jax.experimental import pallas as pl
from jax.experimental.pallas import tpu as pltpu

F32 = jnp.float32
BF16 = jnp.bfloat16
ACT = BF16

HEAD_DIM = 64
LANES = 128
SUBLANES = 8
D_CONV = 256
CONV_WIDTH = 31
N_DN_HEADS = 6
D_DN = N_DN_HEADS * HEAD_DIM
SHORT_CONV = 5
DN_CHUNK = 64
N_Q_HEADS = 6
N_KV_HEADS = 2
D_ATTN = N_Q_HEADS * HEAD_DIM
D_KV = N_KV_HEADS * HEAD_DIM
WINDOW = 128
ATTN_BLOCK = 128
ROPE_BASE = 10000.0
GRID_W = 64
EPS = 1e-6
N_PAIRS = N_DN_HEADS // 2
HALO = 16
NEG = -1e30
VMEM_LIMIT = 48 * 1024 * 1024
GDN_BATCH_ROWS = 4

N_A = 3 * D_CONV
N_QKV = 3 * D_DN
N_SC = 4 * N_DN_HEADS
N_KVC = 2 * D_KV
N_ZZ = D_DN + D_ATTN
OFF_QKV = N_A
OFF_SC = OFF_QKV + N_QKV
OFF_QC = OFF_SC + LANES
OFF_KVC = OFF_QC + D_ATTN
OFF_ZZ = OFF_KVC + N_KVC
N_IN_PAD = OFF_ZZ + N_ZZ
Q_PERM = (0, 3, 1, 4, 2, 5)


def _silu(x):
    return x * jax.nn.sigmoid(x)


def _mm(a, b):
    return jnp.dot(a.astype(BF16), b.astype(BF16), preferred_element_type=F32)


def _mm_nt(a, b):
    return lax.dot_general(a.astype(BF16), b.astype(BF16), (((1,), (1,)), ((), ())), preferred_element_type=F32)


def _mm_tn(a, b):
    return lax.dot_general(a.astype(BF16), b.astype(BF16), (((0,), (0,)), ((), ())), preferred_element_type=F32)


def _split2(x):
    hi = x.astype(BF16)
    lo = (x - hi.astype(F32)).astype(BF16)
    return hi, lo


def _mm_x2(a, b01):
    hi, lo = _split2(a)
    return (jnp.dot(hi, b01, preferred_element_type=F32) + jnp.dot(lo, b01, preferred_element_type=F32))


def _head_sums(x):
    r = lax.broadcasted_iota(jnp.int32, (LANES, LANES), 0) // HEAD_DIM
    c = lax.broadcasted_iota(jnp.int32, (LANES, LANES), 1) // HEAD_DIM
    ones_bd = jnp.where(r == c, 1.0, 0.0).astype(BF16)
    return _mm_x2(x, ones_bd)


def _ada_body(cond_ref, w_ref, b_ref, o_ref):
    a = _silu(cond_ref[...])
    o_ref[0] = jnp.dot(a, w_ref[0], preferred_element_type=F32, precision=lax.Precision.HIGHEST) + b_ref[0]


def _ada_call(cond, ada_w, ada_b):
    depth, d, d3 = ada_w.shape
    rows = cond.shape[0]
    tn = 1024
    return pl.pallas_call(
        _ada_body,
        grid=(depth, d3 // tn),
        in_specs=[pl.BlockSpec((rows, d), lambda l, j: (0, 0)),
                  pl.BlockSpec((1, d, tn), lambda l, j: (l, 0, j)),
                  pl.BlockSpec((1, 1, tn), lambda l, j: (l, 0, j))],
        out_specs=pl.BlockSpec((1, rows, tn), lambda l, j: (l, 0, j)),
        out_shape=jax.ShapeDtypeStruct((depth, rows, d3), F32),
        compiler_params=pltpu.CompilerParams(vmem_limit_bytes=VMEM_LIMIT),
        name="ada",
    )(cond, ada_w, ada_b.reshape(depth, 1, d3))


def _inproj_body(x_ref, mod_ref, nw_ref, w_ref, lanep_ref, cos_ref, sin_ref,
                 pa_ref, pqkv_ref, gb_ref, qc_ref, kvc_ref, zz_ref, *, rope):
    x = x_ref[0]
    y = x * lax.rsqrt(jnp.mean(x * x, axis=-1, keepdims=True) + EPS) * nw_ref[...]
    h = (y * (1.0 + mod_ref[0, 1:2, :]) + mod_ref[0, 0:1, :]).astype(BF16)

    def proj(off, n):
        return jnp.dot(h, w_ref[:, off:off + n], preferred_element_type=F32)

    pa_ref[0] = proj(0, N_A).astype(ACT)
    pqkv_ref[0] = proj(OFF_QKV, N_QKV).astype(ACT)
    zz_ref[0] = proj(OFF_ZZ, N_ZZ).astype(ACT)

    s = proj(OFF_SC, LANES)
    a_log, dt_b, is_decay = lanep_ref[0:1, :], lanep_ref[1:2, :], lanep_ref[2:3, :] > 0.5
    t = s + dt_b
    softplus = jnp.maximum(t, 0.0) + jnp.log(1.0 + jnp.exp(-jnp.abs(t)))
    gb_ref[0] = jnp.where(is_decay, -jnp.exp(a_log) * softplus, jax.nn.sigmoid(s))

    qk = proj(OFF_QC, D_ATTN + D_KV)
    v = proj(OFF_KVC + D_KV, D_KV)
    lane = lax.broadcasted_iota(jnp.int32, (1, LANES), 1)
    first = (lane % (HEAD_DIM // 2)) < (HEAD_DIM // 4)
    for t_ in range((D_ATTN + D_KV) // LANES):
        tile = qk[:, t_ * LANES:(t_ + 1) * LANES]
        if rope:
            partner = jnp.where(first, pltpu.roll(tile, LANES - HEAD_DIM // 4, 1), pltpu.roll(tile, HEAD_DIM // 4, 1))
            tile = tile * cos_ref[...] + partner * sin_ref[...]
        if t_ < D_ATTN // LANES:
            qc_ref[0, :, t_ * LANES:(t_ + 1) * LANES] = (tile * (HEAD_DIM ** -0.5)).astype(ACT)
        else:
            kvc_ref[0, :, 0:D_KV] = tile.astype(ACT)
    kvc_ref[0, :, D_KV:] = v.astype(ACT)


def _inproj_call(x, mod3, norm_w, w_p, lanep, cos_t, sin_t, *, rope, tm):
    b, l, d = x.shape
    outs = (N_A, N_QKV, LANES, D_ATTN, N_KVC, N_ZZ)
    tok = lambda n: pl.BlockSpec((1, tm, n), lambda bi, i: (bi, i, 0))
    return pl.pallas_call(
        functools.partial(_inproj_body, rope=rope),
        grid=(b, l // tm),
        in_specs=[tok(d),
                  pl.BlockSpec((1, 3, d), lambda bi, i: (bi, 0, 0)),
                  pl.BlockSpec((1, d), lambda bi, i: (0, 0)),
                  pl.BlockSpec((d, N_IN_PAD), lambda bi, i: (0, 0)),
                  pl.BlockSpec((8, LANES), lambda bi, i: (0, 0)),
                  pl.BlockSpec((tm, LANES), lambda bi, i: (i, 0)),
                  pl.BlockSpec((tm, LANES), lambda bi, i: (i, 0))],
        out_specs=[tok(n) for n in outs],
        out_shape=[jax.ShapeDtypeStruct((b, l, n), F32 if n == LANES else ACT) for n in outs],
        compiler_params=pltpu.CompilerParams(dimension_semantics=("parallel", "parallel"),
                                             vmem_limit_bytes=VMEM_LIMIT),
        name="inproj_rope" if rope else "inproj",
    )(x, mod3, norm_w, w_p, lanep, cos_t, sin_t)


def _fill_ext(ext_ref, main, prev, nxt, i, n_tiles, tile):
    ext_ref[0:HALO, :] = jnp.where(i > 0, prev, 0.0)
    ext_ref[HALO:HALO + tile, :] = main
    ext_ref[HALO + tile:, :] = jnp.where(i < n_tiles - 1, nxt, 0.0)


def _dwconv(ext_ref, w_ref, width, rows, r0):
    pad = (width - 1) // 2
    acc = None
    for k in range(width):
        term = ext_ref[r0 + HALO - pad + k:r0 + HALO - pad + k + rows, :] * w_ref[k:k + 1, :]
        acc = term if acc is None else acc + term
    return acc


def _dwconv_shifted(sh_ref, w_ref, width, rows, r0):
    pad = (width - 1) // 2
    acc = None
    for k in range(width):
        off = r0 + HALO - pad + k
        r = off % SUBLANES
        term = sh_ref[r, off - r:off - r + rows, :] * w_ref[k:k + 1, :]
        acc = term if acc is None else acc + term
    return acc


def _mixprep_body(pa_ref, pap_ref, pan_ref, q_ref, qp_ref, qn_ref, caw_ref, cab_ref, lnw_ref, lnb_ref, qcw_ref,
                  ya_ref, qkvn_ref, exta_ref, extq_ref, *, tile, n_tiles):
    i = pl.program_id(1)
    glu = lambda t: t[:, 0:D_CONV].astype(F32) * jax.nn.sigmoid(t[:, D_CONV:2 * D_CONV].astype(F32))
    _fill_ext(exta_ref.at[0], glu(pa_ref[0]), glu(pap_ref[0]), glu(pan_ref[0]), i, n_tiles, tile)
    _fill_ext(extq_ref, q_ref[0].astype(F32), qp_ref[0].astype(F32), qn_ref[0].astype(F32), i, n_tiles, tile)
    n_sh = tile + 2 * HALO - SUBLANES
    for r in range(1, SUBLANES):
        exta_ref[r, 0:n_sh, :] = exta_ref[0, r:r + n_sh, :]
    sub = 32
    for r0 in range(0, tile, sub):
        u = _dwconv_shifted(exta_ref, caw_ref, CONV_WIDTH, sub, r0) + cab_ref[...]
        mu = jnp.mean(u, axis=-1, keepdims=True)
        var = jnp.mean(jnp.square(u - mu), axis=-1, keepdims=True)
        u = (u - mu) * lax.rsqrt(var + EPS) * lnw_ref[...] + lnb_ref[...]
        ya_ref[0, r0:r0 + sub, :] = (_silu(u) * _silu(pa_ref[0, r0:r0 + sub, 2 * D_CONV:].astype(F32))).astype(ACT)
    for r0 in range(0, tile, sub):
        t = _silu(_dwconv(extq_ref, qcw_ref, SHORT_CONV, sub, r0))
        for c0 in range(0, N_QKV, LANES):
            tl = t[:, c0:c0 + LANES]
            if c0 < 2 * D_DN:
                tl = tl * lax.rsqrt(_head_sums(tl * tl) + EPS)
                if c0 < D_DN:
                    tl = tl * (HEAD_DIM ** -0.5)
            qkvn_ref[0, r0:r0 + sub, c0:c0 + LANES] = tl.astype(ACT)


def _mixprep_call(pa, pqkv, conv_a_w, conv_a_b, ln_w, ln_b, qkv_conv_w, *, tile):
    b, l, _ = pa.shape
    n_tiles = l // tile
    hpt = tile // HALO
    n_h = l // HALO
    main = lambda n: pl.BlockSpec((1, tile, n), lambda bi, i: (bi, i, 0))
    prev = lambda n: pl.BlockSpec((1, HALO, n), lambda bi, i: (bi, jnp.maximum(i * hpt - 1, 0), 0))
    nxt = lambda n: pl.BlockSpec((1, HALO, n), lambda bi, i: (bi, jnp.minimum((i + 1) * hpt, n_h - 1), 0))
    full = lambda a: pl.BlockSpec(a.shape, lambda bi, i: (0, 0))
    consts = (conv_a_w, conv_a_b.reshape(1, -1), ln_w.reshape(1, -1), ln_b.reshape(1, -1), qkv_conv_w)
    return pl.pallas_call(
        functools.partial(_mixprep_body, tile=tile, n_tiles=n_tiles),
        grid=(b, n_tiles),
        in_specs=[main(N_A), prev(N_A), nxt(N_A), main(N_QKV), prev(N_QKV), nxt(N_QKV)] + [full(a) for a in consts],
        out_specs=[main(D_CONV), main(N_QKV)],
        out_shape=[jax.ShapeDtypeStruct((b, l, D_CONV), ACT), jax.ShapeDtypeStruct((b, l, N_QKV), ACT)],
        scratch_shapes=[pltpu.VMEM((SUBLANES, tile + 2 * HALO, D_CONV), F32),
                        pltpu.VMEM((tile + 2 * HALO, N_QKV), F32)],
        compiler_params=pltpu.CompilerParams(dimension_semantics=("parallel", "parallel"),
                                             vmem_limit_bytes=VMEM_LIMIT),
        name="mixprep",
    )(pa, pa, pa, pqkv, pqkv, pqkv, *consts)


def _blockdiag(x2, bm):
    return jnp.where(bm, jnp.concatenate([x2, x2], axis=0), 0.0)


def _gdn_chunk(dirs, s_ref):
    c = DN_CHUNK
    ri = lax.broadcasted_iota(jnp.int32, (c, LANES), 0)
    cj = lax.broadcasted_iota(jnp.int32, (c, LANES), 1) % HEAD_DIM
    blk16 = (ri // 16) == (cj // 16)
    blk32 = (ri // 32) == (cj // 32)
    eye = jnp.where(ri == cj, 1.0, 0.0)
    bm = (lax.broadcasted_iota(jnp.int32, (LANES, LANES), 0) // HEAD_DIM
          == lax.broadcasted_iota(jnp.int32, (LANES, LANES), 1) // HEAD_DIM)
    ti = lax.broadcasted_iota(jnp.int32, (c, c), 0)
    tt = lax.broadcasted_iota(jnp.int32, (c, c), 1)
    feat = lax.broadcasted_iota(jnp.int32, (1, LANES), 1)
    is_decay = (feat % (2 * N_DN_HEADS)) >= N_DN_HEADS
    lane_lo = feat < HEAD_DIM
    tn = (((0,), (0,)), ((), ()))

    per_dir = {}
    for bi, d, qkv, gb, o_ref in dirs:
        tri = jnp.where((tt <= ti) if d == 0 else (tt >= ti), 1.0, 0.0).astype(BF16)
        tri_t2 = jnp.where((ri <= cj) if d == 0 else (ri >= cj), 1.0, 0.0).astype(BF16)
        g_hi, g_lo = _split2(gb)
        csum = jnp.dot(tri, g_hi, preferred_element_type=F32) + jnp.dot(tri, g_lo, preferred_element_type=F32)
        crow = (lax.dot_general(g_hi, tri_t2, tn, preferred_element_type=F32)
                + lax.dot_general(g_lo, tri_t2, tn, preferred_element_type=F32))
        src = jnp.where(is_decay, csum, gb)
        incl = (ri >= cj) if d == 0 else (ri <= cj)
        strict = (ri > cj) if d == 0 else (ri < cj)
        per_dir[bi, d] = (qkv, o_ref, crow, src, incl, strict)

    chains = [(bi, d, p) for bi, d, _, _, _ in dirs for p in range(N_PAIRS)]
    each = lambda f, *cols: [f(*args) for args in zip(*cols)]

    def load(bi, d, p):
        qkv = per_dir[bi, d][0]
        return tuple(qkv[:, part * D_DN + p * LANES:part * D_DN + (p + 1) * LANES].astype(F32) for part in range(3))

    q2, k2, v2 = zip(*[load(*ch) for ch in chains])

    def spread(bi, d, p, kind):
        src = per_dir[bi, d][3]
        f = d * 2 * N_DN_HEADS + kind * N_DN_HEADS + 2 * p
        return jnp.where(lane_lo, src[:, f:f + 1], src[:, f + 1:f + 2])

    beta2 = [spread(*ch, 0) for ch in chains]
    c2 = [spread(*ch, 1) for ch in chains]

    def decay(ch, c2_):
        bi, d, p = ch
        crow, incl = per_dir[bi, d][2], per_dir[bi, d][4]
        f0 = d * 2 * N_DN_HEADS + N_DN_HEADS + 2 * p
        crow2 = jnp.where(lane_lo, crow[f0:f0 + 1, :], crow[f0 + 1:f0 + 2, :])
        return jnp.exp(jnp.where(incl, c2_ - crow2, NEG))

    dec = each(decay, chains, c2)
    tot = [c2_[c - 1:c, :] if d == 0 else c2_[0:1, :] for (_, d, _), c2_ in zip(chains, c2)]
    ec2 = [jnp.exp(t) for t in c2]
    gq = each(lambda k, q: _mm_nt(jnp.concatenate([k, q], axis=0), _blockdiag(k, bm)), k2, q2)
    m2 = [jnp.where(per_dir[bi, d][5], -(g[:c] * b_ * dc), 0.0)
          for (bi, d, _), g, b_, dc in zip(chains, gq, beta2, dec)]
    intra = [g[c:] * dc for g, dc in zip(gq, dec)]

    pmul = lambda a2, b2: _mm(a2, _blockdiag(b2, bm))
    md = [jnp.where(blk16, m, 0.0) for m in m2]
    tinv = [eye + m for m in md]
    pw = each(pmul, md, md)
    for _ in range(2):
        both = each(lambda t, x: pmul(jnp.concatenate([t, x], axis=0), x), tinv, pw)
        tinv = each(lambda t, y: t + y[:c], tinv, both)
        pw = [y[c:] for y in both]
    tinv = each(lambda t, x: t + pmul(t, x), tinv, pw)
    for off in ([jnp.where(blk32 & ~blk16, m, 0.0) for m in m2], [jnp.where(blk32, 0.0, m) for m in m2]):
        half = each(pmul, tinv, off)
        tinv = each(lambda t, h: t + pmul(h, t), tinv, half)
    rhs = each(lambda v, k, b_, e: jnp.concatenate([_blockdiag(v * b_, bm), _blockdiag(k * b_ * e, bm)], axis=1),
               v2, k2, beta2, ec2)
    uw = each(_mm, tinv, rhs)
    kdec = each(lambda k, t, c2_: k * jnp.exp(t - c2_), k2, tot, c2)
    s_old = [s_ref[bi, d, p] for bi, d, p in chains]
    wq = each(lambda u, q, e, s: _mm(jnp.concatenate([u[:, LANES:], q * e], axis=0), s), uw, q2, ec2, s_old)
    vnew = each(lambda u, w: u[:, :LANES] - w[:c], uw, wq)
    o2 = each(lambda w, a, v: w[c:] + _mm(a, _blockdiag(v, bm)), wq, intra, vnew)
    s_new = each(lambda s, t, k, v: s * jnp.exp(t) + jnp.where(bm, _mm_tn(k, v), 0.0), s_old, tot, kdec, vnew)
    for (bi, d, p), o, s in zip(chains, o2, s_new):
        per_dir[bi, d][1][bi, :, p * LANES:(p + 1) * LANES] = o.astype(ACT)
        s_ref[bi, d, p] = s


def _gdn_body(qkvf_ref, qkvb_ref, gbf_ref, gbb_ref, s0_ref, of_ref, ob_ref, sfin_ref, s_ref, *, n_chunks, bb):
    n = pl.program_id(1)

    @pl.when(n == 0)
    def _():
        s_ref[...] = s0_ref[...]

    dirs = []
    for bi in range(bb):
        dirs += [(bi, 0, qkvf_ref[bi], gbf_ref[bi], of_ref), (bi, 1, qkvb_ref[bi], gbb_ref[bi], ob_ref)]
    _gdn_chunk(dirs, s_ref)

    @pl.when(n == n_chunks - 1)
    def _():
        sfin_ref[...] = s_ref[...]


def _gdn_call(qkvn, gb, s0):
    b, l, _ = qkvn.shape
    c = DN_CHUNK
    nc = l // c
    bb = _pick_tile(b, GDN_BATCH_ROWS)
    fwd = lambda n_: pl.BlockSpec((bb, c, n_), lambda bi, n: (bi, n, 0))
    bwd = lambda n_: pl.BlockSpec((bb, c, n_), lambda bi, n: (bi, nc - 1 - n, 0))
    st = pl.BlockSpec((bb, 2, N_PAIRS, LANES, LANES), lambda bi, n: (bi, 0, 0, 0, 0))
    return pl.pallas_call(
        functools.partial(_gdn_body, n_chunks=nc, bb=bb),
        grid=(b // bb, nc),
        in_specs=[fwd(N_QKV), bwd(N_QKV), fwd(LANES), bwd(LANES), st],
        out_specs=[fwd(D_DN), bwd(D_DN), st],
        out_shape=[jax.ShapeDtypeStruct((b, l, D_DN), ACT), jax.ShapeDtypeStruct((b, l, D_DN), ACT),
                   jax.ShapeDtypeStruct(s0.shape, F32)],
        scratch_shapes=[pltpu.VMEM((bb, 2, N_PAIRS, LANES, LANES), F32)],
        compiler_params=pltpu.CompilerParams(dimension_semantics=("parallel", "arbitrary"),
                                             vmem_limit_bytes=VMEM_LIMIT),
        name="gdn",
    )(qkvn, qkvn, gb, gb, s0)


def _attend(q, keys, vals, mask, sink_ref, o_ref):
    lane_lo = lax.broadcasted_iota(jnp.int32, (1, LANES), 1) < HEAD_DIM
    nq = q.shape[0]
    stacked = [jnp.where(lane_lo if half == 0 else ~lane_lo, q[:, t * LANES:(t + 1) * LANES], 0.0).astype(BF16)
               for t in range(D_ATTN // LANES) for half in range(2)]
    s_all = _mm_nt(jnp.concatenate(stacked, axis=0), keys)
    probs, dens = [], []
    for j in range(N_Q_HEADS):
        sink = sink_ref[Q_PERM[j]]
        s = s_all[j * nq:(j + 1) * nq]
        if mask is not None:
            s = jnp.where(mask, s, NEG)
        m = jnp.maximum(jnp.max(s, axis=-1, keepdims=True), sink)
        pr = jnp.exp(s - m)
        dens.append(jnp.sum(pr, axis=-1, keepdims=True) + jnp.exp(sink - m))
        probs.append(pr.astype(BF16))
    o_all = _mm(jnp.concatenate(probs, axis=0), vals)
    for t in range(D_ATTN // LANES):
        lo = o_all[2 * t * nq:(2 * t + 1) * nq] / dens[2 * t]
        hi = o_all[(2 * t + 1) * nq:(2 * t + 2) * nq] / dens[2 * t + 1]
        o_ref[0, :, t * LANES:(t + 1) * LANES] = jnp.where(lane_lo, lo, hi).astype(ACT)


def _attn_latent_body(sink_ref, q_ref, kvp_ref, kvc_ref, kvn_ref, kvx_ref, o_ref, *, n_blocks):
    i = pl.program_id(1)
    blk = ATTN_BLOCK
    lc = kvx_ref.shape[1]
    kv = jnp.concatenate([kvp_ref[0], kvc_ref[0], kvn_ref[0], kvx_ref[0]], axis=0)
    t = lax.broadcasted_iota(jnp.int32, (blk, 3 * blk + lc), 0)
    u = lax.broadcasted_iota(jnp.int32, (blk, 3 * blk + lc), 1)
    lo = jnp.where(i > 0, 0, blk)
    hi = jnp.where(i < n_blocks - 1, 3 * blk, 2 * blk)
    valid = (u >= 3 * blk) | ((jnp.abs(u - blk - t) <= WINDOW) & (u >= lo) & (u < hi))
    _attend(q_ref[0], kv[:, :D_KV], kv[:, D_KV:], valid, sink_ref, o_ref)


def _attn_ctx_body(sink_ref, q_ref, kvx_ref, o_ref):
    kv = kvx_ref[0]
    _attend(q_ref[0], kv[:, :D_KV], kv[:, D_KV:], None, sink_ref, o_ref)


def _attn_latent_call(qc, kvc, kvx, sink):
    b, l, _ = qc.shape
    lc = kvx.shape[1]
    nb = l // ATTN_BLOCK
    kvspec = lambda f: pl.BlockSpec((1, ATTN_BLOCK, N_KVC), lambda bi, i: (bi, f(i), 0))
    return pl.pallas_call(
        functools.partial(_attn_latent_body, n_blocks=nb),
        grid=(b, nb),
        in_specs=[pl.BlockSpec(memory_space=pltpu.SMEM),
                  pl.BlockSpec((1, ATTN_BLOCK, D_ATTN), lambda bi, i: (bi, i, 0)),
                  kvspec(lambda i: jnp.maximum(i - 1, 0)), kvspec(lambda i: i),
                  kvspec(lambda i: jnp.minimum(i + 1, nb - 1)),
                  pl.BlockSpec((1, lc, N_KVC), lambda bi, i: (bi, 0, 0))],
        out_specs=pl.BlockSpec((1, ATTN_BLOCK, D_ATTN), lambda bi, i: (bi, i, 0)),
        out_shape=jax.ShapeDtypeStruct((b, l, D_ATTN), ACT),
        compiler_params=pltpu.CompilerParams(dimension_semantics=("parallel", "parallel"),
                                             vmem_limit_bytes=VMEM_LIMIT),
        name="attn_latent",
    )(sink, qc, kvc, kvc, kvc, kvx)


def _attn_ctx_call(qcx, kvx, sink):
    b, lc, _ = qcx.shape
    return pl.pallas_call(
        _attn_ctx_body,
        grid=(b, lc // ATTN_BLOCK),
        in_specs=[pl.BlockSpec(memory_space=pltpu.SMEM),
                  pl.BlockSpec((1, ATTN_BLOCK, D_ATTN), lambda bi, i: (bi, i, 0)),
                  pl.BlockSpec((1, lc, N_KVC), lambda bi, i: (bi, 0, 0))],
        out_specs=pl.BlockSpec((1, ATTN_BLOCK, D_ATTN), lambda bi, i: (bi, i, 0)),
        out_shape=jax.ShapeDtypeStruct((b, lc, D_ATTN), ACT),
        compiler_params=pltpu.CompilerParams(dimension_semantics=("parallel", "parallel"),
                                             vmem_limit_bytes=VMEM_LIMIT),
        name="attn_ctx",
    )(sink, qcx, kvx)


def _outproj_body(ya_ref, of_ref, ob_ref, yc_ref, zz_ref, x_ref, mod_ref, dnw_ref, w_ref, fnw_ref, o_ref, *, final):
    acc = jnp.dot(ya_ref[0].astype(BF16), w_ref[0:D_CONV, :], preferred_element_type=F32)
    o = of_ref[0].astype(F32) + ob_ref[0].astype(F32)
    for t in range(D_DN // LANES):
        ot = o[:, t * LANES:(t + 1) * LANES]
        ms = _head_sums(ot * ot) * (1.0 / HEAD_DIM)
        yb = ot * lax.rsqrt(ms + EPS) * dnw_ref[...] * _silu(zz_ref[0, :, t * LANES:(t + 1) * LANES].astype(F32))
        r0 = D_CONV + t * LANES
        acc = acc + jnp.dot(yb.astype(BF16), w_ref[r0:r0 + LANES, :], preferred_element_type=F32)
    yc = yc_ref[0].astype(F32) * _silu(zz_ref[0, :, D_DN:].astype(F32))
    acc = acc + jnp.dot(yc.astype(BF16), w_ref[D_CONV + D_DN:, :], preferred_element_type=F32)
    xn = x_ref[0] + mod_ref[0, 2:3, :] * acc
    if final:
        xn = xn * lax.rsqrt(jnp.mean(xn * xn, axis=-1, keepdims=True) + EPS) * fnw_ref[...]
    o_ref[0] = xn


def _outproj_call(ya, o_f, o_b, yc, zz, x, mod3, dnw, w_p, fnw, *, final, tm):
    b, l, d = x.shape
    tok = lambda n: pl.BlockSpec((1, tm, n), lambda bi, i: (bi, i, 0))
    return pl.pallas_call(
        functools.partial(_outproj_body, final=final),
        grid=(b, l // tm),
        in_specs=[tok(D_CONV), tok(D_DN), tok(D_DN), tok(D_ATTN), tok(N_ZZ), tok(d),
                  pl.BlockSpec((1, 3, d), lambda bi, i: (bi, 0, 0)),
                  pl.BlockSpec((1, LANES), lambda bi, i: (0, 0)),
                  pl.BlockSpec(w_p.shape, lambda bi, i: (0, 0)),
                  pl.BlockSpec((1, d), lambda bi, i: (0, 0))],
        out_specs=tok(d),
        out_shape=jax.ShapeDtypeStruct((b, l, d), F32),
        compiler_params=pltpu.CompilerParams(dimension_semantics=("parallel", "parallel"),
                                             vmem_limit_bytes=VMEM_LIMIT),
        name="outproj_final" if final else "outproj",
    )(ya, o_f, o_b, yc, zz, x, mod3, dnw, w_p, fnw)


def _permute_heads(w, axis):
    parts = jnp.split(w, N_Q_HEADS, axis=axis)
    return jnp.concatenate([parts[h] for h in Q_PERM], axis=axis)


def _layout_w_in(w):
    sizes = (D_CONV, D_CONV, D_CONV, D_DN, D_DN, D_DN, D_DN, N_SC, D_ATTN, D_KV, D_KV, D_ATTN)
    (a_val, a_gate, z_a, q_b, k_b, v_b, z_b, sc, q_c, k_c, v_c, z_c) = jnp.split(w, np.cumsum(sizes)[:-1].tolist(), axis=1)
    sc_pad = jnp.pad(sc, ((0, 0), (0, LANES - N_SC)))
    cols = [a_val, a_gate, z_a, q_b, k_b, v_b, sc_pad, _permute_heads(q_c, 1), k_c, v_c, z_b, _permute_heads(z_c, 1)]
    return jnp.concatenate(cols, axis=1).astype(BF16)


def _layout_w_out(w):
    return jnp.concatenate([w[:D_CONV + D_DN], _permute_heads(w[D_CONV + D_DN:], 0)], axis=0).astype(BF16)


def _lane_params(a_log, dt_bias):
    z = jnp.zeros((2, N_DN_HEADS), F32)
    one = jnp.ones((2, N_DN_HEADS), F32)
    rows = [jnp.stack([z, a_log], axis=1), jnp.stack([z, dt_bias], axis=1), jnp.stack([z, one], axis=1)]
    rows = jnp.stack([r.reshape(N_SC) for r in rows], axis=0)
    return jnp.pad(rows, ((0, 8 - rows.shape[0]), (0, LANES - N_SC)))


def _rope_tables(n_tok):
    quarter = HEAD_DIM // 4
    inv = ROPE_BASE ** (-jnp.arange(quarter, dtype=F32) / quarter)
    pos = jnp.arange(n_tok, dtype=jnp.int32)
    rows, cols = (pos // GRID_W).astype(F32), (pos % GRID_W).astype(F32)
    ang = jnp.concatenate([rows[:, None] * inv, rows[:, None] * inv, cols[:, None] * inv, cols[:, None] * inv], axis=1)
    sign = jnp.tile(jnp.concatenate([-jnp.ones(quarter, F32), jnp.ones(quarter, F32)]), 2)
    cos, sin = jnp.cos(ang), jnp.sin(ang) * sign
    return jnp.tile(cos, (1, LANES // HEAD_DIM)), jnp.tile(sin, (1, LANES // HEAD_DIM))


def _pick_tile(n, pref):
    t = min(pref, n)
    while n % t:
        t //= 2
    return t


def kernel(x, c, ctx, c_ctx, norm_w, ada_w, ada_b, w_in, conv_a_w, conv_a_b, ln_a_w, ln_a_b, qkv_conv_w, a_log,
           dt_bias, dn_norm_w, sink, w_out, final_norm_w):
    b, l, d = x.shape
    lc = ctx.shape[1]
    depth = w_in.shape[0]
    assert l % ATTN_BLOCK == 0 and lc % ATTN_BLOCK == 0 and l % GRID_W == 0

    rows = -(-(b + 1) // 8) * 8
    cond = jnp.zeros((rows, d), F32).at[:b].set(c).at[b].set(c_ctx)
    mod = _ada_call(cond, ada_w, ada_b).reshape(depth, rows, 3, d)
    cos_t, sin_t = _rope_tables(l)
    s_zero = jnp.zeros((b, 2, N_PAIRS, LANES, LANES), F32)
    tm = _pick_tile(l, 512)
    tmc = _pick_tile(lc, 256)

    xc = ctx
    for li in range(depth):
        last = li == depth - 1
        mod_x = mod[li, :b]
        mod_c = jnp.broadcast_to(mod[li, b][None], (b, 3, d))
        w_p = _layout_w_in(w_in[li])
        wo_p = _layout_w_out(w_out[li])
        lanep = _lane_params(a_log[li], dt_bias[li])
        nw = norm_w[li].reshape(1, d)
        dnw = jnp.tile(dn_norm_w[li], LANES // HEAD_DIM).reshape(1, LANES)
        fnw = final_norm_w.reshape(1, d)
        mix = (conv_a_w[li], conv_a_b[li], ln_a_w[li], ln_a_b[li], qkv_conv_w[li])

        pa, pqkv, gb, qc, kvc, zz = _inproj_call(x, mod_x, nw, w_p, lanep, cos_t, sin_t, rope=True, tm=tm)
        pa_x, pqkv_x, gb_x, qc_x, kvc_x, zz_x = _inproj_call(xc, mod_c, nw, w_p, lanep, cos_t[:lc], sin_t[:lc],
                                                             rope=False, tm=tmc)
        ya, qkvn = _mixprep_call(pa, pqkv, *mix, tile=_pick_tile(l, 256))
        ya_x, qkvn_x = _mixprep_call(pa_x, pqkv_x, *mix, tile=tmc)
        of_x, ob_x, s_ctx = _gdn_call(qkvn_x, gb_x, s_zero)
        o_f, o_b, _ = _gdn_call(qkvn, gb, s_ctx)
        yc = _attn_latent_call(qc, kvc, kvc_x, sink[li])
        x = _outproj_call(ya, o_f, o_b, yc, zz, x, mod_x, dnw, wo_p, fnw, final=last, tm=tm)
        if not last:
            yc_x = _attn_ctx_call(qc_x, kvc_x, sink[li])
            xc = _outproj_call(ya_x, of_x, ob_x, yc_x, zz_x, xc, mod_c, dnw, wo_p, fnw, final=False, tm=tmc)
    return x
```

```python
import functools

import jax
import jax.numpy as jnp
import numpy as np
from jax import lax
from jax.experimental import pallas as pl
from jax.experimental.pallas import tpu as pltpu

F32 = jnp.float32
BF16 = jnp.bfloat16
ACT = BF16

HEAD_DIM = 64
LANES = 128
SUBLANES = 8
D_CONV = 256
CONV_WIDTH = 31
N_DN_HEADS = 6
D_DN = N_DN_HEADS * HEAD_DIM
SHORT_CONV = 5
DN_CHUNK = 64
N_Q_HEADS = 6
N_KV_HEADS = 2
D_ATTN = N_Q_HEADS * HEAD_DIM
D_KV = N_KV_HEADS * HEAD_DIM
WINDOW = 128
ATTN_BLOCK = 128
ROPE_BASE = 10000.0
GRID_W = 64
EPS = 1e-6
N_PAIRS = N_DN_HEADS // 2
HALO = 16
NEG = -1e30
VMEM_LIMIT = 48 * 1024 * 1024
ATTN_BLOCKS_PER_STEP = 4
GDN_BATCH_ROWS = 4

N_A = 3 * D_CONV
N_QKV = 3 * D_DN
N_SC = 4 * N_DN_HEADS
N_KVC = 2 * D_KV
N_ZZ = D_DN + D_ATTN
OFF_QKV = N_A
OFF_SC = OFF_QKV + N_QKV
OFF_QC = OFF_SC + LANES
OFF_KVC = OFF_QC + D_ATTN
OFF_ZZ = OFF_KVC + N_KVC
N_IN_PAD = OFF_ZZ + N_ZZ
Q_PERM = (0, 3, 1, 4, 2, 5)


def _silu(x):
    return x * jax.nn.sigmoid(x)


def _mm(a, b):
    return jnp.dot(a.astype(BF16), b.astype(BF16), preferred_element_type=F32)


def _mm_nt(a, b):
    return lax.dot_general(a.astype(BF16), b.astype(BF16), (((1,), (1,)), ((), ())), preferred_element_type=F32)


def _mm_tn(a, b):
    return lax.dot_general(a.astype(BF16), b.astype(BF16), (((0,), (0,)), ((), ())), preferred_element_type=F32)


def _split2(x):
    hi = x.astype(BF16)
    lo = (x - hi.astype(F32)).astype(BF16)
    return hi, lo


def _mm_x2(a, b01):
    hi, lo = _split2(a)
    return (jnp.dot(hi, b01, preferred_element_type=F32) + jnp.dot(lo, b01, preferred_element_type=F32))


def _head_sums(x):
    r = lax.broadcasted_iota(jnp.int32, (LANES, LANES), 0) // HEAD_DIM
    c = lax.broadcasted_iota(jnp.int32, (LANES, LANES), 1) // HEAD_DIM
    ones_bd = jnp.where(r == c, 1.0, 0.0).astype(BF16)
    return _mm_x2(x, ones_bd)


def _ada_body(cond_ref, w_ref, b_ref, o_ref):
    a = _silu(cond_ref[...])
    o_ref[0] = jnp.dot(a, w_ref[0], preferred_element_type=F32, precision=lax.Precision.HIGHEST) + b_ref[0]


def _ada_call(cond, ada_w, ada_b):
    depth, d, d3 = ada_w.shape
    rows = cond.shape[0]
    tn = 1024
    return pl.pallas_call(
        _ada_body,
        grid=(depth, d3 // tn),
        in_specs=[pl.BlockSpec((rows, d), lambda l, j: (0, 0)),
                  pl.BlockSpec((1, d, tn), lambda l, j: (l, 0, j)),
                  pl.BlockSpec((1, 1, tn), lambda l, j: (l, 0, j))],
        out_specs=pl.BlockSpec((1, rows, tn), lambda l, j: (l, 0, j)),
        out_shape=jax.ShapeDtypeStruct((depth, rows, d3), F32),
        compiler_params=pltpu.CompilerParams(vmem_limit_bytes=VMEM_LIMIT),
        name="ada",
    )(cond, ada_w, ada_b.reshape(depth, 1, d3))


def _inproj_body(x_ref, mod_ref, nw_ref, w_ref, lanep_ref, cos_ref, sin_ref,
                 pa_ref, pqkv_ref, gb_ref, qc_ref, kvc_ref, zz_ref, *, rope):
    x = x_ref[0]
    y = x * lax.rsqrt(jnp.mean(x * x, axis=-1, keepdims=True) + EPS) * nw_ref[...]
    h = (y * (1.0 + mod_ref[0, 1:2, :]) + mod_ref[0, 0:1, :]).astype(BF16)

    def proj(off, n):
        return jnp.dot(h, w_ref[:, off:off + n], preferred_element_type=F32)

    pa_ref[0] = proj(0, N_A).astype(ACT)
    pqkv_ref[0] = proj(OFF_QKV, N_QKV).astype(ACT)
    zz_ref[0] = proj(OFF_ZZ, N_ZZ).astype(ACT)

    s = proj(OFF_SC, LANES)
    a_log, dt_b, is_decay = lanep_ref[0:1, :], lanep_ref[1:2, :], lanep_ref[2:3, :] > 0.5
    t = s + dt_b
    softplus = jnp.maximum(t, 0.0) + jnp.log(1.0 + jnp.exp(-jnp.abs(t)))
    gb_ref[0] = jnp.where(is_decay, -jnp.exp(a_log) * softplus, jax.nn.sigmoid(s))

    qk = proj(OFF_QC, D_ATTN + D_KV)
    v = proj(OFF_KVC + D_KV, D_KV)
    lane = lax.broadcasted_iota(jnp.int32, (1, LANES), 1)
    first = (lane % (HEAD_DIM // 2)) < (HEAD_DIM // 4)
    for t_ in range((D_ATTN + D_KV) // LANES):
        tile = qk[:, t_ * LANES:(t_ + 1) * LANES]
        if rope:
            partner = jnp.where(first, pltpu.roll(tile, LANES - HEAD_DIM // 4, 1), pltpu.roll(tile, HEAD_DIM // 4, 1))
            tile = tile * cos_ref[...] + partner * sin_ref[...]
        if t_ < D_ATTN // LANES:
            qc_ref[0, :, t_ * LANES:(t_ + 1) * LANES] = (tile * (HEAD_DIM ** -0.5)).astype(ACT)
        else:
            kvc_ref[0, :, 0:D_KV] = tile.astype(ACT)
    kvc_ref[0, :, D_KV:] = v.astype(ACT)


def _inproj_call(x, mod3, norm_w, w_p, lanep, cos_t, sin_t, *, rope, tm):
    b, l, d = x.shape
    outs = (N_A, N_QKV, LANES, D_ATTN, N_KVC, N_ZZ)
    tok = lambda n: pl.BlockSpec((1, tm, n), lambda bi, i: (bi, i, 0))
    return pl.pallas_call(
        functools.partial(_inproj_body, rope=rope),
        grid=(b, l // tm),
        in_specs=[tok(d),
                  pl.BlockSpec((1, 3, d), lambda bi, i: (bi, 0, 0)),
                  pl.BlockSpec((1, d), lambda bi, i: (0, 0)),
                  pl.BlockSpec((d, N_IN_PAD), lambda bi, i: (0, 0)),
                  pl.BlockSpec((8, LANES), lambda bi, i: (0, 0)),
                  pl.BlockSpec((tm, LANES), lambda bi, i: (i, 0)),
                  pl.BlockSpec((tm, LANES), lambda bi, i: (i, 0))],
        out_specs=[tok(n) for n in outs],
        out_shape=[jax.ShapeDtypeStruct((b, l, n), F32 if n == LANES else ACT) for n in outs],
        compiler_params=pltpu.CompilerParams(dimension_semantics=("parallel", "parallel"),
                                             vmem_limit_bytes=VMEM_LIMIT),
        name="inproj_rope" if rope else "inproj",
    )(x, mod3, norm_w, w_p, lanep, cos_t, sin_t)


def _fill_ext(ext_ref, main, prev, nxt, i, n_tiles, tile):
    ext_ref[0:HALO, :] = jnp.where(i > 0, prev, 0.0)
    ext_ref[HALO:HALO + tile, :] = main
    ext_ref[HALO + tile:, :] = jnp.where(i < n_tiles - 1, nxt, 0.0)


def _dwconv_shifted(sh_ref, w_ref, width, rows, r0):
    pad = (width - 1) // 2
    acc = None
    for k in range(width):
        off = r0 + HALO - pad + k
        r = off % SUBLANES
        term = sh_ref[r, off - r:off - r + rows, :] * w_ref[k:k + 1, :]
        acc = term if acc is None else acc + term
    return acc


def _mixprep_body(pa_ref, pap_ref, pan_ref, q_ref, qp_ref, qn_ref, caw_ref, cab_ref, lnw_ref, lnb_ref, qcw_ref,
                  ya_ref, qkvn_ref, exta_ref, extq_ref, *, tile, n_tiles):
    i = pl.program_id(1)
    glu = lambda t: t[:, 0:D_CONV].astype(F32) * jax.nn.sigmoid(t[:, D_CONV:2 * D_CONV].astype(F32))
    _fill_ext(exta_ref.at[0], glu(pa_ref[0]), glu(pap_ref[0]), glu(pan_ref[0]), i, n_tiles, tile)
    _fill_ext(extq_ref, q_ref[0], qp_ref[0], qn_ref[0], i, n_tiles, tile)
    n_sh = tile + 2 * HALO - SUBLANES
    for r in range(1, SUBLANES):
        exta_ref[r, 0:n_sh, :] = exta_ref[0, r:r + n_sh, :]
    sub = 32

    def conformer_block(r0):
        u = _dwconv_shifted(exta_ref, caw_ref, CONV_WIDTH, sub, r0) + cab_ref[...]
        mu = jnp.mean(u, axis=-1, keepdims=True)
        var = jnp.mean(jnp.square(u - mu), axis=-1, keepdims=True)
        u = (u - mu) * lax.rsqrt(var + EPS) * lnw_ref[...] + lnb_ref[...]
        ya_ref[0, r0:r0 + sub, :] = (_silu(u) * _silu(pa_ref[0, r0:r0 + sub, 2 * D_CONV:].astype(F32))).astype(ACT)

    win = 2 * sub
    lead = HALO - (SHORT_CONV - 1) // 2
    assert sub + lead + SHORT_CONV - 1 <= win and ACT == BF16
    sr = lax.broadcasted_iota(jnp.int32, (SHORT_CONV * sub, win), 0)
    sc = lax.broadcasted_iota(jnp.int32, (SHORT_CONV * sub, win), 1)
    shift = jnp.where(sc == (sr % sub) + lead + sr // sub, 1.0, 0.0).astype(ACT)
    shifted = lambda r0: jnp.dot(shift, extq_ref[r0:r0 + win, :], preferred_element_type=F32)
    nxt_taps = shifted(0)
    for r0 in range(0, tile, sub):
        taps = nxt_taps
        if r0 + sub < tile:
            nxt_taps = shifted(r0 + sub)
        conformer_block(r0)
        acc = None
        for k in range(SHORT_CONV):
            term = taps[k * sub:(k + 1) * sub] * qcw_ref[k:k + 1, :]
            acc = term if acc is None else acc + term
        t = _silu(acc)
        for c0 in range(0, N_QKV, LANES):
            tl = t[:, c0:c0 + LANES]
            if c0 < 2 * D_DN:
                tl = tl * lax.rsqrt(_head_sums(tl * tl) + EPS)
                if c0 < D_DN:
                    tl = tl * (HEAD_DIM ** -0.5)
            qkvn_ref[0, r0:r0 + sub, c0:c0 + LANES] = tl.astype(ACT)


def _mixprep_call(pa, pqkv, conv_a_w, conv_a_b, ln_w, ln_b, qkv_conv_w, *, tile):
    b, l, _ = pa.shape
    n_tiles = l // tile
    hpt = tile // HALO
    n_h = l // HALO
    main = lambda n: pl.BlockSpec((1, tile, n), lambda bi, i: (bi, i, 0))
    prev = lambda n: pl.BlockSpec((1, HALO, n), lambda bi, i: (bi, jnp.maximum(i * hpt - 1, 0), 0))
    nxt = lambda n: pl.BlockSpec((1, HALO, n), lambda bi, i: (bi, jnp.minimum((i + 1) * hpt, n_h - 1), 0))
    full = lambda a: pl.BlockSpec(a.shape, lambda bi, i: (0, 0))
    consts = (conv_a_w, conv_a_b.reshape(1, -1), ln_w.reshape(1, -1), ln_b.reshape(1, -1), qkv_conv_w)
    return pl.pallas_call(
        functools.partial(_mixprep_body, tile=tile, n_tiles=n_tiles),
        grid=(b, n_tiles),
        in_specs=[main(N_A), prev(N_A), nxt(N_A), main(N_QKV), prev(N_QKV), nxt(N_QKV)] + [full(a) for a in consts],
        out_specs=[main(D_CONV), main(N_QKV)],
        out_shape=[jax.ShapeDtypeStruct((b, l, D_CONV), ACT), jax.ShapeDtypeStruct((b, l, N_QKV), ACT)],
        scratch_shapes=[pltpu.VMEM((SUBLANES, tile + 2 * HALO, D_CONV), F32),
                        pltpu.VMEM((tile + 2 * HALO, N_QKV), ACT)],
        compiler_params=pltpu.CompilerParams(dimension_semantics=("parallel", "parallel"),
                                             vmem_limit_bytes=VMEM_LIMIT),
        name="mixprep",
    )(pa, pa, pa, pqkv, pqkv, pqkv, *consts)


def _blockdiag(x2, bm):
    return jnp.where(bm, jnp.concatenate([x2, x2], axis=0), 0.0)


def _gdn_chunk(dirs, s_ref):
    c = DN_CHUNK
    ri = lax.broadcasted_iota(jnp.int32, (c, LANES), 0)
    cj = lax.broadcasted_iota(jnp.int32, (c, LANES), 1) % HEAD_DIM
    blk16 = (ri // 16) == (cj // 16)
    blk32 = (ri // 32) == (cj // 32)
    eye = jnp.where(ri == cj, 1.0, 0.0)
    bm = (lax.broadcasted_iota(jnp.int32, (LANES, LANES), 0) // HEAD_DIM
          == lax.broadcasted_iota(jnp.int32, (LANES, LANES), 1) // HEAD_DIM)
    ti = lax.broadcasted_iota(jnp.int32, (c, c), 0)
    tt = lax.broadcasted_iota(jnp.int32, (c, c), 1)
    feat = lax.broadcasted_iota(jnp.int32, (1, LANES), 1)
    is_decay = (feat % (2 * N_DN_HEADS)) >= N_DN_HEADS
    lane_lo = feat < HEAD_DIM
    tn = (((0,), (0,)), ((), ()))

    per_dir = {}
    for bi, d, qkv, gb, o_ref in dirs:
        tri = jnp.where((tt <= ti) if d == 0 else (tt >= ti), 1.0, 0.0).astype(BF16)
        tri_t2 = jnp.where((ri <= cj) if d == 0 else (ri >= cj), 1.0, 0.0).astype(BF16)
        g_hi, g_lo = _split2(gb)
        csum = jnp.dot(tri, g_hi, preferred_element_type=F32) + jnp.dot(tri, g_lo, preferred_element_type=F32)
        crow = (lax.dot_general(g_hi, tri_t2, tn, preferred_element_type=F32)
                + lax.dot_general(g_lo, tri_t2, tn, preferred_element_type=F32))
        src = jnp.where(is_decay, csum, gb)
        incl = (ri >= cj) if d == 0 else (ri <= cj)
        strict = (ri > cj) if d == 0 else (ri < cj)
        per_dir[bi, d] = (qkv, o_ref, crow, src, incl, strict)

    chains = [(bi, d, p) for bi, d, _, _, _ in dirs for p in range(N_PAIRS)]
    each = lambda f, *cols: [f(*args) for args in zip(*cols)]

    def load(bi, d, p):
        qkv = per_dir[bi, d][0]
        return tuple(qkv[:, part * D_DN + p * LANES:part * D_DN + (p + 1) * LANES].astype(F32) for part in range(3))

    q2, k2, v2 = zip(*[load(*ch) for ch in chains])

    def spread(bi, d, p, kind):
        src = per_dir[bi, d][3]
        f = d * 2 * N_DN_HEADS + kind * N_DN_HEADS + 2 * p
        return jnp.where(lane_lo, src[:, f:f + 1], src[:, f + 1:f + 2])

    beta2 = [spread(*ch, 0) for ch in chains]
    c2 = [spread(*ch, 1) for ch in chains]

    def decay(ch, c2_):
        bi, d, p = ch
        crow, incl = per_dir[bi, d][2], per_dir[bi, d][4]
        f0 = d * 2 * N_DN_HEADS + N_DN_HEADS + 2 * p
        crow2 = jnp.where(lane_lo, crow[f0:f0 + 1, :], crow[f0 + 1:f0 + 2, :])
        return jnp.exp(jnp.where(incl, c2_ - crow2, NEG))

    dec = each(decay, chains, c2)
    tot = [c2_[c - 1:c, :] if d == 0 else c2_[0:1, :] for (_, d, _), c2_ in zip(chains, c2)]
    ec2 = [jnp.exp(t) for t in c2]
    gq = each(lambda k, q: _mm_nt(jnp.concatenate([k, q], axis=0), _blockdiag(k, bm)), k2, q2)
    m2 = [jnp.where(per_dir[bi, d][5], -(g[:c] * b_ * dc), 0.0)
          for (bi, d, _), g, b_, dc in zip(chains, gq, beta2, dec)]
    intra = [g[c:] * dc for g, dc in zip(gq, dec)]

    pmul = lambda a2, b2: _mm(a2, _blockdiag(b2, bm))
    md = [jnp.where(blk16, m, 0.0) for m in m2]
    tinv = [eye + m for m in md]
    pw = each(pmul, md, md)
    for _ in range(2):
        both = each(lambda t, x: pmul(jnp.concatenate([t, x], axis=0), x), tinv, pw)
        tinv = each(lambda t, y: t + y[:c], tinv, both)
        pw = [y[c:] for y in both]
    tinv = each(lambda t, x: t + pmul(t, x), tinv, pw)
    for off in ([jnp.where(blk32 & ~blk16, m, 0.0) for m in m2], [jnp.where(blk32, 0.0, m) for m in m2]):
        half = each(pmul, tinv, off)
        tinv = each(lambda t, h: t + pmul(h, t), tinv, half)
    rhs = each(lambda v, k, b_, e: jnp.concatenate([_blockdiag(v * b_, bm), _blockdiag(k * b_ * e, bm)], axis=1),
               v2, k2, beta2, ec2)
    uw = each(_mm, tinv, rhs)
    kdec = each(lambda k, t, c2_: k * jnp.exp(t - c2_), k2, tot, c2)
    s_old = [s_ref[bi, d, p] for bi, d, p in chains]
    wq = each(lambda u, q, e, s: _mm(jnp.concatenate([u[:, LANES:], q * e], axis=0), s), uw, q2, ec2, s_old)
    vnew = each(lambda u, w: u[:, :LANES] - w[:c], uw, wq)
    o2 = each(lambda w, a, v: w[c:] + _mm(a, _blockdiag(v, bm)), wq, intra, vnew)
    s_new = each(lambda s, t, k, v: s * jnp.exp(t) + jnp.where(bm, _mm_tn(k, v), 0.0), s_old, tot, kdec, vnew)
    for (bi, d, p), o, s in zip(chains, o2, s_new):
        per_dir[bi, d][1][bi, :, p * LANES:(p + 1) * LANES] = o.astype(ACT)
        s_ref[bi, d, p] = s


def _gdn_body(qkvf_ref, qkvb_ref, gbf_ref, gbb_ref, s0_ref, of_ref, ob_ref, sfin_ref, s_ref, *, n_chunks, bb):
    n = pl.program_id(1)

    @pl.when(n == 0)
    def _():
        s_ref[...] = s0_ref[...]

    dirs = []
    for bi in range(bb):
        dirs += [(bi, 0, qkvf_ref[bi], gbf_ref[bi], of_ref), (bi, 1, qkvb_ref[bi], gbb_ref[bi], ob_ref)]
    _gdn_chunk(dirs, s_ref)

    @pl.when(n == n_chunks - 1)
    def _():
        sfin_ref[...] = s_ref[...]


def _gdn_call(qkvn, gb, s0):
    b, l, _ = qkvn.shape
    c = DN_CHUNK
    nc = l // c
    bb = _pick_tile(b, GDN_BATCH_ROWS)
    fwd = lambda n_: pl.BlockSpec((bb, c, n_), lambda bi, n: (bi, n, 0))
    bwd = lambda n_: pl.BlockSpec((bb, c, n_), lambda bi, n: (bi, nc - 1 - n, 0))
    st = pl.BlockSpec((bb, 2, N_PAIRS, LANES, LANES), lambda bi, n: (bi, 0, 0, 0, 0))
    return pl.pallas_call(
        functools.partial(_gdn_body, n_chunks=nc, bb=bb),
        grid=(b // bb, nc),
        in_specs=[fwd(N_QKV), bwd(N_QKV), fwd(LANES), bwd(LANES), st],
        out_specs=[fwd(D_DN), bwd(D_DN), st],
        out_shape=[jax.ShapeDtypeStruct((b, l, D_DN), ACT), jax.ShapeDtypeStruct((b, l, D_DN), ACT),
                   jax.ShapeDtypeStruct(s0.shape, F32)],
        scratch_shapes=[pltpu.VMEM((bb, 2, N_PAIRS, LANES, LANES), F32)],
        compiler_params=pltpu.CompilerParams(dimension_semantics=("parallel", "arbitrary"),
                                             vmem_limit_bytes=VMEM_LIMIT),
        name="gdn",
    )(qkvn, qkvn, gb, gb, s0)


def _attend(problems, sink_ref):
    lane_lo = lax.broadcasted_iota(jnp.int32, (1, LANES), 1) < HEAD_DIM

    def stack_heads(q):
        return jnp.concatenate(
            [jnp.where(lane_lo if half == 0 else ~lane_lo, q[:, t * LANES:(t + 1) * LANES], 0.0).astype(BF16)
             for t in range(D_ATTN // LANES) for half in range(2)], axis=0)

    scores = [_mm_nt(stack_heads(q), keys) for q, keys, _, _ in problems]
    soft = []
    for (q, _, _, mask), s_all in zip(problems, scores):
        nq = q.shape[0]
        probs, dens = [], []
        for j in range(N_Q_HEADS):
            sink = sink_ref[Q_PERM[j]]
            s = s_all[j * nq:(j + 1) * nq]
            if mask is not None:
                s = jnp.where(mask, s, NEG)
            m = jnp.maximum(jnp.max(s, axis=-1, keepdims=True), sink)
            pr = jnp.exp(s - m)
            dens.append(jnp.sum(pr, axis=-1, keepdims=True) + jnp.exp(sink - m))
            probs.append(pr.astype(BF16))
        soft.append((jnp.concatenate(probs, axis=0), dens))
    outs = []
    for (q, _, vals, _), (probs, dens) in zip(problems, soft):
        nq = q.shape[0]
        o_all = _mm(probs, vals)
        outs.append([jnp.where(lane_lo, o_all[2 * t * nq:(2 * t + 1) * nq] / dens[2 * t],
                               o_all[(2 * t + 1) * nq:(2 * t + 2) * nq] / dens[2 * t + 1])
                     for t in range(D_ATTN // LANES)])
    return outs


def _attn_latent_body(sink_ref, q_ref, kvp_ref, kvc_ref, kvn_ref, kvx_ref, o_ref, *, n_steps, sub):
    i = pl.program_id(1)
    blk = ATTN_BLOCK
    lc = kvx_ref.shape[1]
    band = jnp.concatenate([kvp_ref[0], kvc_ref[0], kvn_ref[0]], axis=0)
    ctx = kvx_ref[0]
    t = lax.broadcasted_iota(jnp.int32, (blk, 3 * blk + lc), 0)
    u = lax.broadcasted_iota(jnp.int32, (blk, 3 * blk + lc), 1)
    problems = []
    for j in range(sub):
        kv = jnp.concatenate([band[j * blk:(j + 3) * blk], ctx], axis=0)
        lo = jnp.where(i > 0, 0, blk) if j == 0 else 0
        hi = jnp.where(i < n_steps - 1, 3 * blk, 2 * blk) if j == sub - 1 else 3 * blk
        valid = (u >= 3 * blk) | ((jnp.abs(u - blk - t) <= WINDOW) & (u >= lo) & (u < hi))
        problems.append((q_ref[0, j * blk:(j + 1) * blk, :], kv[:, :D_KV], kv[:, D_KV:], valid))
    for j, tiles in enumerate(_attend(problems, sink_ref)):
        for t_, tile in enumerate(tiles):
            o_ref[0, j * blk:(j + 1) * blk, t_ * LANES:(t_ + 1) * LANES] = tile.astype(ACT)


def _attn_ctx_body(sink_ref, q_ref, kvx_ref, o_ref):
    kv = kvx_ref[0]
    (tiles,) = _attend([(q_ref[0], kv[:, :D_KV], kv[:, D_KV:], None)], sink_ref)
    for t_, tile in enumerate(tiles):
        o_ref[0, :, t_ * LANES:(t_ + 1) * LANES] = tile.astype(ACT)


def _attn_latent_call(qc, kvc, kvx, sink):
    b, l, _ = qc.shape
    lc = kvx.shape[1]
    blk = ATTN_BLOCK
    nb = l // blk
    sub = _pick_tile(nb, ATTN_BLOCKS_PER_STEP)
    edge = lambda f: pl.BlockSpec((1, blk, N_KVC), lambda bi, i: (bi, f(i), 0))
    return pl.pallas_call(
        functools.partial(_attn_latent_body, n_steps=nb // sub, sub=sub),
        grid=(b, nb // sub),
        in_specs=[pl.BlockSpec(memory_space=pltpu.SMEM),
                  pl.BlockSpec((1, sub * blk, D_ATTN), lambda bi, i: (bi, i, 0)),
                  edge(lambda i: jnp.maximum(sub * i - 1, 0)),
                  pl.BlockSpec((1, sub * blk, N_KVC), lambda bi, i: (bi, i, 0)),
                  edge(lambda i: jnp.minimum(sub * (i + 1), nb - 1)),
                  pl.BlockSpec((1, lc, N_KVC), lambda bi, i: (bi, 0, 0))],
        out_specs=pl.BlockSpec((1, sub * blk, D_ATTN), lambda bi, i: (bi, i, 0)),
        out_shape=jax.ShapeDtypeStruct((b, l, D_ATTN), ACT),
        compiler_params=pltpu.CompilerParams(dimension_semantics=("parallel", "parallel"),
                                             vmem_limit_bytes=VMEM_LIMIT),
        name="attn_latent",
    )(sink, qc, kvc, kvc, kvc, kvx)


def _attn_ctx_call(qcx, kvx, sink):
    b, lc, _ = qcx.shape
    return pl.pallas_call(
        _attn_ctx_body,
        grid=(b, lc // ATTN_BLOCK),
        in_specs=[pl.BlockSpec(memory_space=pltpu.SMEM),
                  pl.BlockSpec((1, ATTN_BLOCK, D_ATTN), lambda bi, i: (bi, i, 0)),
                  pl.BlockSpec((1, lc, N_KVC), lambda bi, i: (bi, 0, 0))],
        out_specs=pl.BlockSpec((1, ATTN_BLOCK, D_ATTN), lambda bi, i: (bi, i, 0)),
        out_shape=jax.ShapeDtypeStruct((b, lc, D_ATTN), ACT),
        compiler_params=pltpu.CompilerParams(dimension_semantics=("parallel", "parallel"),
                                             vmem_limit_bytes=VMEM_LIMIT),
        name="attn_ctx",
    )(sink, qcx, kvx)


def _outproj_body(ya_ref, of_ref, ob_ref, yc_ref, zz_ref, x_ref, mod_ref, dnw_ref, w_ref, fnw_ref, o_ref, *, final):
    acc = jnp.dot(ya_ref[0].astype(BF16), w_ref[0:D_CONV, :], preferred_element_type=F32)
    o = of_ref[0].astype(F32) + ob_ref[0].astype(F32)
    for t in range(D_DN // LANES):
        ot = o[:, t * LANES:(t + 1) * LANES]
        ms = _head_sums(ot * ot) * (1.0 / HEAD_DIM)
        yb = ot * lax.rsqrt(ms + EPS) * dnw_ref[...] * _silu(zz_ref[0, :, t * LANES:(t + 1) * LANES].astype(F32))
        r0 = D_CONV + t * LANES
        acc = acc + jnp.dot(yb.astype(BF16), w_ref[r0:r0 + LANES, :], preferred_element_type=F32)
    yc = yc_ref[0].astype(F32) * _silu(zz_ref[0, :, D_DN:].astype(F32))
    acc = acc + jnp.dot(yc.astype(BF16), w_ref[D_CONV + D_DN:, :], preferred_element_type=F32)
    xn = x_ref[0] + mod_ref[0, 2:3, :] * acc
    if final:
        xn = xn * lax.rsqrt(jnp.mean(xn * xn, axis=-1, keepdims=True) + EPS) * fnw_ref[...]
    o_ref[0] = xn


def _outproj_call(ya, o_f, o_b, yc, zz, x, mod3, dnw, w_p, fnw, *, final, tm):
    b, l, d = x.shape
    tok = lambda n: pl.BlockSpec((1, tm, n), lambda bi, i: (bi, i, 0))
    return pl.pallas_call(
        functools.partial(_outproj_body, final=final),
        grid=(b, l // tm),
        in_specs=[tok(D_CONV), tok(D_DN), tok(D_DN), tok(D_ATTN), tok(N_ZZ), tok(d),
                  pl.BlockSpec((1, 3, d), lambda bi, i: (bi, 0, 0)),
                  pl.BlockSpec((1, LANES), lambda bi, i: (0, 0)),
                  pl.BlockSpec(w_p.shape, lambda bi, i: (0, 0)),
                  pl.BlockSpec((1, d), lambda bi, i: (0, 0))],
        out_specs=tok(d),
        out_shape=jax.ShapeDtypeStruct((b, l, d), F32),
        compiler_params=pltpu.CompilerParams(dimension_semantics=("parallel", "parallel"),
                                             vmem_limit_bytes=VMEM_LIMIT),
        name="outproj_final" if final else "outproj",
    )(ya, o_f, o_b, yc, zz, x, mod3, dnw, w_p, fnw)


def _permute_heads(w, axis):
    parts = jnp.split(w, N_Q_HEADS, axis=axis)
    return jnp.concatenate([parts[h] for h in Q_PERM], axis=axis)


def _layout_w_in(w):
    sizes = (D_CONV, D_CONV, D_CONV, D_DN, D_DN, D_DN, D_DN, N_SC, D_ATTN, D_KV, D_KV, D_ATTN)
    (a_val, a_gate, z_a, q_b, k_b, v_b, z_b, sc, q_c, k_c, v_c, z_c) = jnp.split(w, np.cumsum(sizes)[:-1].tolist(), axis=1)
    sc_pad = jnp.pad(sc, ((0, 0), (0, LANES - N_SC)))
    cols = [a_val, a_gate, z_a, q_b, k_b, v_b, sc_pad, _permute_heads(q_c, 1), k_c, v_c, z_b, _permute_heads(z_c, 1)]
    return jnp.concatenate(cols, axis=1).astype(BF16)


def _layout_w_out(w):
    return jnp.concatenate([w[:D_CONV + D_DN], _permute_heads(w[D_CONV + D_DN:], 0)], axis=0).astype(BF16)


def _lane_params(a_log, dt_bias):
    z = jnp.zeros((2, N_DN_HEADS), F32)
    one = jnp.ones((2, N_DN_HEADS), F32)
    rows = [jnp.stack([z, a_log], axis=1), jnp.stack([z, dt_bias], axis=1), jnp.stack([z, one], axis=1)]
    rows = jnp.stack([r.reshape(N_SC) for r in rows], axis=0)
    return jnp.pad(rows, ((0, 8 - rows.shape[0]), (0, LANES - N_SC)))


def _rope_tables(n_tok):
    quarter = HEAD_DIM // 4
    inv = ROPE_BASE ** (-jnp.arange(quarter, dtype=F32) / quarter)
    pos = jnp.arange(n_tok, dtype=jnp.int32)
    rows, cols = (pos // GRID_W).astype(F32), (pos % GRID_W).astype(F32)
    ang = jnp.concatenate([rows[:, None] * inv, rows[:, None] * inv, cols[:, None] * inv, cols[:, None] * inv], axis=1)
    sign = jnp.tile(jnp.concatenate([-jnp.ones(quarter, F32), jnp.ones(quarter, F32)]), 2)
    cos, sin = jnp.cos(ang), jnp.sin(ang) * sign
    return jnp.tile(cos, (1, LANES // HEAD_DIM)), jnp.tile(sin, (1, LANES // HEAD_DIM))


def _pick_tile(n, pref):
    t = min(pref, n)
    while n % t:
        t //= 2
    return t


def kernel(x, c, ctx, c_ctx, norm_w, ada_w, ada_b, w_in, conv_a_w, conv_a_b, ln_a_w, ln_a_b, qkv_conv_w, a_log,
           dt_bias, dn_norm_w, sink, w_out, final_norm_w):
    b, l, d = x.shape
    lc = ctx.shape[1]
    depth = w_in.shape[0]
    assert l % ATTN_BLOCK == 0 and lc % ATTN_BLOCK == 0 and l % GRID_W == 0

    rows = -(-(b + 1) // 8) * 8
    cond = jnp.zeros((rows, d), F32).at[:b].set(c).at[b].set(c_ctx)
    mod = _ada_call(cond, ada_w, ada_b).reshape(depth, rows, 3, d)
    cos_t, sin_t = _rope_tables(l)
    s_zero = jnp.zeros((b, 2, N_PAIRS, LANES, LANES), F32)
    tm = _pick_tile(l, 512)
    tmc = _pick_tile(lc, 256)

    xc = ctx
    for li in range(depth):
        last = li == depth - 1
        mod_x = mod[li, :b]
        mod_c = jnp.broadcast_to(mod[li, b][None], (b, 3, d))
        w_p = _layout_w_in(w_in[li])
        wo_p = _layout_w_out(w_out[li])
        lanep = _lane_params(a_log[li], dt_bias[li])
        nw = norm_w[li].reshape(1, d)
        dnw = jnp.tile(dn_norm_w[li], LANES // HEAD_DIM).reshape(1, LANES)
        fnw = final_norm_w.reshape(1, d)
        mix = (conv_a_w[li], conv_a_b[li], ln_a_w[li], ln_a_b[li], qkv_conv_w[li])

        pa, pqkv, gb, qc, kvc, zz = _inproj_call(x, mod_x, nw, w_p, lanep, cos_t, sin_t, rope=True, tm=tm)
        pa_x, pqkv_x, gb_x, qc_x, kvc_x, zz_x = _inproj_call(xc, mod_c, nw, w_p, lanep, cos_t[:lc], sin_t[:lc],
                                                             rope=False, tm=tmc)
        ya, qkvn = _mixprep_call(pa, pqkv, *mix, tile=_pick_tile(l, 256))
        ya_x, qkvn_x = _mixprep_call(pa_x, pqkv_x, *mix, tile=tmc)
        of_x, ob_x, s_ctx = _gdn_call(qkvn_x, gb_x, s_zero)
        o_f, o_b, _ = _gdn_call(qkvn, gb, s_ctx)
        yc = _attn_latent_call(qc, kvc, kvc_x, sink[li])
        x = _outproj_call(ya, o_f, o_b, yc, zz, x, mod_x, dnw, wo_p, fnw, final=last, tm=tm)
        if not last:
            yc_x = _attn_ctx_call(qc_x, kvc_x, sink[li])
            xc = _outproj_call(ya_x, of_x, ob_x, yc_x, zz_x, xc, mod_c, dnw, wo_p, fnw, final=False, tm=tmc)
    return x
```

```python
import functools

import jax
import jax.numpy as jnp
import numpy as np
from jax import lax
from jax.experimental import pallas as pl
from jax.experimental.pallas import tpu as pltpu

F32 = jnp.float32
BF16 = jnp.bfloat16
ACT = BF16

HEAD_DIM = 64
LANES = 128
SUBLANES = 8
D_CONV = 256
CONV_WIDTH = 31
N_DN_HEADS = 6
D_DN = N_DN_HEADS * HEAD_DIM
SHORT_CONV = 5
DN_CHUNK = 64
N_Q_HEADS = 6
N_KV_HEADS = 2
D_ATTN = N_Q_HEADS * HEAD_DIM
D_KV = N_KV_HEADS * HEAD_DIM
WINDOW = 128
ATTN_BLOCK = 128
ROPE_BASE = 10000.0
GRID_W = 64
EPS = 1e-6
N_PAIRS = N_DN_HEADS // 2
HALO = 16
NEG = -1e30
VMEM_LIMIT = 48 * 1024 * 1024
MIX_ROWS = 128
ATTN_BLOCKS_PER_STEP = 4
GDN_BATCH_ROWS = 4

N_A = 3 * D_CONV
N_QKV = 3 * D_DN
N_SC = 4 * N_DN_HEADS
N_KVC = 2 * D_KV
N_ZZ = D_DN + D_ATTN
OFF_QKV = N_A
OFF_SC = OFF_QKV + N_QKV
OFF_QC = OFF_SC + LANES
OFF_KVC = OFF_QC + D_ATTN
OFF_ZZ = OFF_KVC + N_KVC
N_IN_PAD = OFF_ZZ + N_ZZ
Q_PERM = (0, 3, 1, 4, 2, 5)


def _silu(x):
    return x * jax.nn.sigmoid(x)


def _mm(a, b):
    return jnp.dot(a.astype(BF16), b.astype(BF16), preferred_element_type=F32)


def _mm_nt(a, b):
    return lax.dot_general(a.astype(BF16), b.astype(BF16), (((1,), (1,)), ((), ())), preferred_element_type=F32)


def _mm_tn(a, b):
    return lax.dot_general(a.astype(BF16), b.astype(BF16), (((0,), (0,)), ((), ())), preferred_element_type=F32)


def _split2(x):
    hi = x.astype(BF16)
    lo = (x - hi.astype(F32)).astype(BF16)
    return hi, lo


def _mm_x2(a, b01):
    hi, lo = _split2(a)
    return (jnp.dot(hi, b01, preferred_element_type=F32) + jnp.dot(lo, b01, preferred_element_type=F32))


def _head_sums(x):
    r = lax.broadcasted_iota(jnp.int32, (LANES, LANES), 0) // HEAD_DIM
    c = lax.broadcasted_iota(jnp.int32, (LANES, LANES), 1) // HEAD_DIM
    ones_bd = jnp.where(r == c, 1.0, 0.0).astype(BF16)
    return _mm_x2(x, ones_bd)


def _ada_body(cond_ref, w_ref, b_ref, o_ref):
    a = _silu(cond_ref[...])
    o_ref[0] = jnp.dot(a, w_ref[0], preferred_element_type=F32, precision=lax.Precision.HIGHEST) + b_ref[0]


def _ada_call(cond, ada_w, ada_b):
    depth, d, d3 = ada_w.shape
    rows = cond.shape[0]
    tn = 1024
    return pl.pallas_call(
        _ada_body,
        grid=(depth, d3 // tn),
        in_specs=[pl.BlockSpec((rows, d), lambda l, j: (0, 0)),
                  pl.BlockSpec((1, d, tn), lambda l, j: (l, 0, j)),
                  pl.BlockSpec((1, 1, tn), lambda l, j: (l, 0, j))],
        out_specs=pl.BlockSpec((1, rows, tn), lambda l, j: (l, 0, j)),
        out_shape=jax.ShapeDtypeStruct((depth, rows, d3), F32),
        compiler_params=pltpu.CompilerParams(vmem_limit_bytes=VMEM_LIMIT),
        name="ada",
    )(cond, ada_w, ada_b.reshape(depth, 1, d3))


def _inproj_body(x_ref, mod_ref, nw_ref, w_ref, lanep_ref, cos_ref, sin_ref,
                 pa_ref, pqkv_ref, gb_ref, qc_ref, kvc_ref, zz_ref, *, rope):
    x = x_ref[0]
    y = x * lax.rsqrt(jnp.mean(x * x, axis=-1, keepdims=True) + EPS) * nw_ref[...]
    h = (y * (1.0 + mod_ref[0, 1:2, :]) + mod_ref[0, 0:1, :]).astype(BF16)

    def proj(off, n):
        return jnp.dot(h, w_ref[:, off:off + n], preferred_element_type=F32)

    pa_ref[0] = proj(0, N_A).astype(ACT)
    pqkv_ref[0] = proj(OFF_QKV, N_QKV).astype(ACT)
    zz_ref[0] = proj(OFF_ZZ, N_ZZ).astype(ACT)

    s = proj(OFF_SC, LANES)
    a_log, dt_b, is_decay = lanep_ref[0:1, :], lanep_ref[1:2, :], lanep_ref[2:3, :] > 0.5
    t = s + dt_b
    softplus = jnp.maximum(t, 0.0) + jnp.log(1.0 + jnp.exp(-jnp.abs(t)))
    gb_ref[0] = jnp.where(is_decay, -jnp.exp(a_log) * softplus, jax.nn.sigmoid(s))

    qk = proj(OFF_QC, D_ATTN + D_KV)
    v = proj(OFF_KVC + D_KV, D_KV)
    lane = lax.broadcasted_iota(jnp.int32, (1, LANES), 1)
    first = (lane % (HEAD_DIM // 2)) < (HEAD_DIM // 4)
    for t_ in range((D_ATTN + D_KV) // LANES):
        tile = qk[:, t_ * LANES:(t_ + 1) * LANES]
        if rope:
            partner = jnp.where(first, pltpu.roll(tile, LANES - HEAD_DIM // 4, 1), pltpu.roll(tile, HEAD_DIM // 4, 1))
            tile = tile * cos_ref[...] + partner * sin_ref[...]
        if t_ < D_ATTN // LANES:
            qc_ref[0, :, t_ * LANES:(t_ + 1) * LANES] = (tile * (HEAD_DIM ** -0.5)).astype(ACT)
        else:
            kvc_ref[0, :, 0:D_KV] = tile.astype(ACT)
    kvc_ref[0, :, D_KV:] = v.astype(ACT)


def _inproj_call(x, mod3, norm_w, w_p, lanep, cos_t, sin_t, *, rope, tm):
    b, l, d = x.shape
    outs = (N_A, N_QKV, LANES, D_ATTN, N_KVC, N_ZZ)
    tok = lambda n: pl.BlockSpec((1, tm, n), lambda bi, i: (bi, i, 0))
    return pl.pallas_call(
        functools.partial(_inproj_body, rope=rope),
        grid=(b, l // tm),
        in_specs=[tok(d),
                  pl.BlockSpec((1, 3, d), lambda bi, i: (bi, 0, 0)),
                  pl.BlockSpec((1, d), lambda bi, i: (0, 0)),
                  pl.BlockSpec((d, N_IN_PAD), lambda bi, i: (0, 0)),
                  pl.BlockSpec((8, LANES), lambda bi, i: (0, 0)),
                  pl.BlockSpec((tm, LANES), lambda bi, i: (i, 0)),
                  pl.BlockSpec((tm, LANES), lambda bi, i: (i, 0))],
        out_specs=[tok(n) for n in outs],
        out_shape=[jax.ShapeDtypeStruct((b, l, n), F32 if n == LANES else ACT) for n in outs],
        compiler_params=pltpu.CompilerParams(dimension_semantics=("parallel", "parallel"),
                                             vmem_limit_bytes=VMEM_LIMIT),
        name="inproj_rope" if rope else "inproj",
    )(x, mod3, norm_w, w_p, lanep, cos_t, sin_t)


def _fill_ext(ext_ref, main, prev, nxt, i, n_tiles, tile):
    ext_ref[0:HALO, :] = jnp.where(i > 0, prev, 0.0)
    ext_ref[HALO:HALO + tile, :] = main
    ext_ref[HALO + tile:, :] = jnp.where(i < n_tiles - 1, nxt, 0.0)


def _dwconv_shifted(sh_ref, w_ref, width, rows, r0):
    pad = (width - 1) // 2
    acc = None
    for k in range(width):
        off = r0 + HALO - pad + k
        r = off % SUBLANES
        term = sh_ref[r, off - r:off - r + rows, :] * w_ref[k:k + 1, :]
        acc = term if acc is None else acc + term
    return acc


def _mixprep_body(pa_ref, pap_ref, pan_ref, q_ref, qp_ref, qn_ref, caw_ref, cab_ref, lnw_ref, lnb_ref, qcw_ref,
                  ya_ref, qkvn_ref, exta_ref, extq_ref, *, tile, n_tiles):
    i = pl.program_id(1)
    glu = lambda t: t[:, 0:D_CONV].astype(F32) * jax.nn.sigmoid(t[:, D_CONV:2 * D_CONV].astype(F32))
    _fill_ext(exta_ref.at[0], glu(pa_ref[0]), glu(pap_ref[0]), glu(pan_ref[0]), i, n_tiles, tile)
    _fill_ext(extq_ref, q_ref[0], qp_ref[0], qn_ref[0], i, n_tiles, tile)
    n_sh = tile + 2 * HALO - SUBLANES
    for r in range(1, SUBLANES):
        exta_ref[r, 0:n_sh, :] = exta_ref[0, r:r + n_sh, :]
    sub = min(MIX_ROWS, tile)

    def conformer_block(r0):
        u = _dwconv_shifted(exta_ref, caw_ref, CONV_WIDTH, sub, r0) + cab_ref[...]
        mu = jnp.mean(u, axis=-1, keepdims=True)
        var = jnp.mean(jnp.square(u - mu), axis=-1, keepdims=True)
        u = (u - mu) * lax.rsqrt(var + EPS) * lnw_ref[...] + lnb_ref[...]
        ya_ref[0, r0:r0 + sub, :] = (_silu(u) * _silu(pa_ref[0, r0:r0 + sub, 2 * D_CONV:].astype(F32))).astype(ACT)

    win = sub + 2 * HALO
    lead = HALO - (SHORT_CONV - 1) // 2
    assert sub + lead + SHORT_CONV - 1 <= win and ACT == BF16
    sr = lax.broadcasted_iota(jnp.int32, (SHORT_CONV * sub, win), 0)
    sc = lax.broadcasted_iota(jnp.int32, (SHORT_CONV * sub, win), 1)
    shift = jnp.where(sc == (sr % sub) + lead + sr // sub, 1.0, 0.0).astype(ACT)
    shifted = lambda r0: jnp.dot(shift, extq_ref[r0:r0 + win, :], preferred_element_type=F32)
    nxt_taps = shifted(0)
    for r0 in range(0, tile, sub):
        taps = nxt_taps
        if r0 + sub < tile:
            nxt_taps = shifted(r0 + sub)
        conformer_block(r0)
        acc = None
        for k in range(SHORT_CONV):
            term = taps[k * sub:(k + 1) * sub] * qcw_ref[k:k + 1, :]
            acc = term if acc is None else acc + term
        t = _silu(acc)
        for c0 in range(0, N_QKV, LANES):
            tl = t[:, c0:c0 + LANES]
            if c0 < 2 * D_DN:
                tl = tl * lax.rsqrt(_head_sums(tl * tl) + EPS)
                if c0 < D_DN:
                    tl = tl * (HEAD_DIM ** -0.5)
            qkvn_ref[0, r0:r0 + sub, c0:c0 + LANES] = tl.astype(ACT)


def _mixprep_call(pa, pqkv, conv_a_w, conv_a_b, ln_w, ln_b, qkv_conv_w, *, tile):
    b, l, _ = pa.shape
    n_tiles = l // tile
    hpt = tile // HALO
    n_h = l // HALO
    main = lambda n: pl.BlockSpec((1, tile, n), lambda bi, i: (bi, i, 0))
    prev = lambda n: pl.BlockSpec((1, HALO, n), lambda bi, i: (bi, jnp.maximum(i * hpt - 1, 0), 0))
    nxt = lambda n: pl.BlockSpec((1, HALO, n), lambda bi, i: (bi, jnp.minimum((i + 1) * hpt, n_h - 1), 0))
    full = lambda a: pl.BlockSpec(a.shape, lambda bi, i: (0, 0))
    consts = (conv_a_w, conv_a_b.reshape(1, -1), ln_w.reshape(1, -1), ln_b.reshape(1, -1), qkv_conv_w)
    return pl.pallas_call(
        functools.partial(_mixprep_body, tile=tile, n_tiles=n_tiles),
        grid=(b, n_tiles),
        in_specs=[main(N_A), prev(N_A), nxt(N_A), main(N_QKV), prev(N_QKV), nxt(N_QKV)] + [full(a) for a in consts],
        out_specs=[main(D_CONV), main(N_QKV)],
        out_shape=[jax.ShapeDtypeStruct((b, l, D_CONV), ACT), jax.ShapeDtypeStruct((b, l, N_QKV), ACT)],
        scratch_shapes=[pltpu.VMEM((SUBLANES, tile + 2 * HALO, D_CONV), F32),
                        pltpu.VMEM((tile + 2 * HALO, N_QKV), ACT)],
        compiler_params=pltpu.CompilerParams(dimension_semantics=("parallel", "parallel"),
                                             vmem_limit_bytes=VMEM_LIMIT),
        name="mixprep",
    )(pa, pa, pa, pqkv, pqkv, pqkv, *consts)


def _blockdiag(x2, bm):
    return jnp.where(bm, jnp.concatenate([x2, x2], axis=0), 0.0)


def _gdn_chunk(dirs, s_ref):
    c = DN_CHUNK
    ri = lax.broadcasted_iota(jnp.int32, (c, LANES), 0)
    cj = lax.broadcasted_iota(jnp.int32, (c, LANES), 1) % HEAD_DIM
    blk16 = (ri // 16) == (cj // 16)
    blk32 = (ri // 32) == (cj // 32)
    eye = jnp.where(ri == cj, 1.0, 0.0)
    bm = (lax.broadcasted_iota(jnp.int32, (LANES, LANES), 0) // HEAD_DIM
          == lax.broadcasted_iota(jnp.int32, (LANES, LANES), 1) // HEAD_DIM)
    ti = lax.broadcasted_iota(jnp.int32, (c, c), 0)
    tt = lax.broadcasted_iota(jnp.int32, (c, c), 1)
    feat = lax.broadcasted_iota(jnp.int32, (1, LANES), 1)
    is_decay = (feat % (2 * N_DN_HEADS)) >= N_DN_HEADS
    lane_lo = feat < HEAD_DIM
    tn = (((0,), (0,)), ((), ()))

    per_dir = {}
    for bi, d, qkv, gb, o_ref in dirs:
        tri = jnp.where((tt <= ti) if d == 0 else (tt >= ti), 1.0, 0.0).astype(BF16)
        tri_t2 = jnp.where((ri <= cj) if d == 0 else (ri >= cj), 1.0, 0.0).astype(BF16)
        g_hi, g_lo = _split2(gb)
        csum = jnp.dot(tri, g_hi, preferred_element_type=F32) + jnp.dot(tri, g_lo, preferred_element_type=F32)
        crow = (lax.dot_general(g_hi, tri_t2, tn, preferred_element_type=F32)
                + lax.dot_general(g_lo, tri_t2, tn, preferred_element_type=F32))
        src = jnp.where(is_decay, csum, gb)
        incl = (ri >= cj) if d == 0 else (ri <= cj)
        strict = (ri > cj) if d == 0 else (ri < cj)
        per_dir[bi, d] = (qkv, o_ref, crow, src, incl, strict)

    chains = [(bi, d, p) for bi, d, _, _, _ in dirs for p in range(N_PAIRS)]
    each = lambda f, *cols: [f(*args) for args in zip(*cols)]

    def load(bi, d, p):
        qkv = per_dir[bi, d][0]
        return tuple(qkv[:, part * D_DN + p * LANES:part * D_DN + (p + 1) * LANES] for part in range(3))

    q_in, k_in, v_in = zip(*[load(*ch) for ch in chains])
    q2, k2, v2 = ([t.astype(F32) for t in col] for col in (q_in, k_in, v_in))

    def spread(bi, d, p, kind):
        src = per_dir[bi, d][3]
        f = d * 2 * N_DN_HEADS + kind * N_DN_HEADS + 2 * p
        return jnp.where(lane_lo, src[:, f:f + 1], src[:, f + 1:f + 2])

    beta2 = [spread(*ch, 0) for ch in chains]
    c2 = [spread(*ch, 1) for ch in chains]

    def decay(ch, c2_):
        bi, d, p = ch
        crow, incl = per_dir[bi, d][2], per_dir[bi, d][4]
        f0 = d * 2 * N_DN_HEADS + N_DN_HEADS + 2 * p
        crow2 = jnp.where(lane_lo, crow[f0:f0 + 1, :], crow[f0 + 1:f0 + 2, :])
        return jnp.exp(jnp.where(incl, c2_ - crow2, NEG))

    dec = each(decay, chains, c2)
    tot = [c2_[c - 1:c, :] if d == 0 else c2_[0:1, :] for (_, d, _), c2_ in zip(chains, c2)]
    ec2 = [jnp.exp(t) for t in c2]
    gq = each(lambda k, q: _mm_nt(jnp.concatenate([k, q], axis=0), _blockdiag(k, bm)), k_in, q_in)
    m2 = [jnp.where(per_dir[bi, d][5], -(g[:c] * b_ * dc), 0.0)
          for (bi, d, _), g, b_, dc in zip(chains, gq, beta2, dec)]
    intra = [g[c:] * dc for g, dc in zip(gq, dec)]

    pmul = lambda a2, b2: _mm(a2, _blockdiag(b2, bm))
    md = [jnp.where(blk16, m, 0.0) for m in m2]
    tinv = [eye + m for m in md]
    pw = each(pmul, md, md)
    for _ in range(2):
        both = each(lambda t, x: pmul(jnp.concatenate([t, x], axis=0), x), tinv, pw)
        tinv = each(lambda t, y: t + y[:c], tinv, both)
        pw = [y[c:] for y in both]
    tinv = each(lambda t, x: t + pmul(t, x), tinv, pw)
    for off in ([jnp.where(blk32 & ~blk16, m, 0.0) for m in m2], [jnp.where(blk32, 0.0, m) for m in m2]):
        half = each(pmul, tinv, off)
        tinv = each(lambda t, h: t + pmul(h, t), tinv, half)
    rhs = each(lambda v, k, b_, e: jnp.concatenate([_blockdiag(v * b_, bm), _blockdiag(k * b_ * e, bm)], axis=1),
               v2, k2, beta2, ec2)
    uw = each(_mm, tinv, rhs)
    kdec = each(lambda k, t, c2_: k * jnp.exp(t - c2_), k2, tot, c2)
    s_old = [s_ref[bi, d, p] for bi, d, p in chains]
    wq = each(lambda u, q, e, s: _mm(jnp.concatenate([u[:, LANES:], q * e], axis=0), s), uw, q2, ec2, s_old)
    vnew = each(lambda u, w: u[:, :LANES] - w[:c], uw, wq)
    o2 = each(lambda w, a, v: w[c:] + _mm(a, _blockdiag(v, bm)), wq, intra, vnew)
    s_new = each(lambda s, t, k, v: s * jnp.exp(t) + jnp.where(bm, _mm_tn(k, v), 0.0), s_old, tot, kdec, vnew)
    for (bi, d, p), o, s in zip(chains, o2, s_new):
        per_dir[bi, d][1][bi, :, p * LANES:(p + 1) * LANES] = o.astype(ACT)
        s_ref[bi, d, p] = s


def _gdn_body(qkvf_ref, qkvb_ref, gbf_ref, gbb_ref, s0_ref, of_ref, ob_ref, sfin_ref, s_ref, *, n_chunks, bb):
    n = pl.program_id(1)

    @pl.when(n == 0)
    def _():
        s_ref[...] = s0_ref[...]

    dirs = []
    for bi in range(bb):
        dirs += [(bi, 0, qkvf_ref[bi], gbf_ref[bi], of_ref), (bi, 1, qkvb_ref[bi], gbb_ref[bi], ob_ref)]
    _gdn_chunk(dirs, s_ref)

    @pl.when(n == n_chunks - 1)
    def _():
        sfin_ref[...] = s_ref[...]


def _gdn_call(qkvn, gb, s0):
    b, l, _ = qkvn.shape
    c = DN_CHUNK
    nc = l // c
    bb = _pick_tile(b, GDN_BATCH_ROWS)
    fwd = lambda n_: pl.BlockSpec((bb, c, n_), lambda bi, n: (bi, n, 0))
    bwd = lambda n_: pl.BlockSpec((bb, c, n_), lambda bi, n: (bi, nc - 1 - n, 0))
    st = pl.BlockSpec((bb, 2, N_PAIRS, LANES, LANES), lambda bi, n: (bi, 0, 0, 0, 0))
    return pl.pallas_call(
        functools.partial(_gdn_body, n_chunks=nc, bb=bb),
        grid=(b // bb, nc),
        in_specs=[fwd(N_QKV), bwd(N_QKV), fwd(LANES), bwd(LANES), st],
        out_specs=[fwd(D_DN), bwd(D_DN), st],
        out_shape=[jax.ShapeDtypeStruct((b, l, D_DN), ACT), jax.ShapeDtypeStruct((b, l, D_DN), ACT),
                   jax.ShapeDtypeStruct(s0.shape, F32)],
        scratch_shapes=[pltpu.VMEM((bb, 2, N_PAIRS, LANES, LANES), F32)],
        compiler_params=pltpu.CompilerParams(dimension_semantics=("parallel", "arbitrary"),
                                             vmem_limit_bytes=VMEM_LIMIT),
        name="gdn",
    )(qkvn, qkvn, gb, gb, s0)


def _attend(problems, sink_ref):
    lane_lo = lax.broadcasted_iota(jnp.int32, (1, LANES), 1) < HEAD_DIM

    def stack_heads(q):
        return jnp.concatenate(
            [jnp.where(lane_lo if half == 0 else ~lane_lo, q[:, t * LANES:(t + 1) * LANES], 0.0).astype(BF16)
             for t in range(D_ATTN // LANES) for half in range(2)], axis=0)

    scores = [_mm_nt(stack_heads(q), keys) for q, keys, _, _ in problems]
    soft = []
    for (q, _, _, mask), s_all in zip(problems, scores):
        nq = q.shape[0]
        probs, dens = [], []
        for j in range(N_Q_HEADS):
            sink = sink_ref[Q_PERM[j]]
            s = s_all[j * nq:(j + 1) * nq]
            if mask is not None:
                s = jnp.where(mask, s, NEG)
            m = jnp.maximum(jnp.max(s, axis=-1, keepdims=True), sink)
            pr = jnp.exp(s - m)
            dens.append(jnp.sum(pr, axis=-1, keepdims=True) + jnp.exp(sink - m))
            probs.append(pr.astype(BF16))
        soft.append((jnp.concatenate(probs, axis=0), dens))
    outs = []
    for (q, _, vals, _), (probs, dens) in zip(problems, soft):
        nq = q.shape[0]
        o_all = _mm(probs, vals)
        outs.append([jnp.where(lane_lo, o_all[2 * t * nq:(2 * t + 1) * nq] / dens[2 * t],
                               o_all[(2 * t + 1) * nq:(2 * t + 2) * nq] / dens[2 * t + 1])
                     for t in range(D_ATTN // LANES)])
    return outs


def _attn_latent_body(sink_ref, q_ref, kvp_ref, kvc_ref, kvn_ref, kvx_ref, o_ref, *, n_steps, sub):
    i = pl.program_id(1)
    blk = ATTN_BLOCK
    lc = kvx_ref.shape[1]
    band = jnp.concatenate([kvp_ref[0], kvc_ref[0], kvn_ref[0]], axis=0)
    ctx = kvx_ref[0]
    t = lax.broadcasted_iota(jnp.int32, (blk, 3 * blk + lc), 0)
    u = lax.broadcasted_iota(jnp.int32, (blk, 3 * blk + lc), 1)
    problems = []
    for j in range(sub):
        kv = jnp.concatenate([band[j * blk:(j + 3) * blk], ctx], axis=0)
        lo = jnp.where(i > 0, 0, blk) if j == 0 else 0
        hi = jnp.where(i < n_steps - 1, 3 * blk, 2 * blk) if j == sub - 1 else 3 * blk
        valid = (u >= 3 * blk) | ((jnp.abs(u - blk - t) <= WINDOW) & (u >= lo) & (u < hi))
        problems.append((q_ref[0, j * blk:(j + 1) * blk, :], kv[:, :D_KV], kv[:, D_KV:], valid))
    for j, tiles in enumerate(_attend(problems, sink_ref)):
        for t_, tile in enumerate(tiles):
            o_ref[0, j * blk:(j + 1) * blk, t_ * LANES:(t_ + 1) * LANES] = tile.astype(ACT)


def _attn_ctx_body(sink_ref, q_ref, kvx_ref, o_ref):
    kv = kvx_ref[0]
    (tiles,) = _attend([(q_ref[0], kv[:, :D_KV], kv[:, D_KV:], None)], sink_ref)
    for t_, tile in enumerate(tiles):
        o_ref[0, :, t_ * LANES:(t_ + 1) * LANES] = tile.astype(ACT)


def _attn_latent_call(qc, kvc, kvx, sink):
    b, l, _ = qc.shape
    lc = kvx.shape[1]
    blk = ATTN_BLOCK
    nb = l // blk
    sub = _pick_tile(nb, ATTN_BLOCKS_PER_STEP)
    edge = lambda f: pl.BlockSpec((1, blk, N_KVC), lambda bi, i: (bi, f(i), 0))
    return pl.pallas_call(
        functools.partial(_attn_latent_body, n_steps=nb // sub, sub=sub),
        grid=(b, nb // sub),
        in_specs=[pl.BlockSpec(memory_space=pltpu.SMEM),
                  pl.BlockSpec((1, sub * blk, D_ATTN), lambda bi, i: (bi, i, 0)),
                  edge(lambda i: jnp.maximum(sub * i - 1, 0)),
                  pl.BlockSpec((1, sub * blk, N_KVC), lambda bi, i: (bi, i, 0)),
                  edge(lambda i: jnp.minimum(sub * (i + 1), nb - 1)),
                  pl.BlockSpec((1, lc, N_KVC), lambda bi, i: (bi, 0, 0))],
        out_specs=pl.BlockSpec((1, sub * blk, D_ATTN), lambda bi, i: (bi, i, 0)),
        out_shape=jax.ShapeDtypeStruct((b, l, D_ATTN), ACT),
        compiler_params=pltpu.CompilerParams(dimension_semantics=("parallel", "parallel"),
                                             vmem_limit_bytes=VMEM_LIMIT),
        name="attn_latent",
    )(sink, qc, kvc, kvc, kvc, kvx)


def _attn_ctx_call(qcx, kvx, sink):
    b, lc, _ = qcx.shape
    return pl.pallas_call(
        _attn_ctx_body,
        grid=(b, lc // ATTN_BLOCK),
        in_specs=[pl.BlockSpec(memory_space=pltpu.SMEM),
                  pl.BlockSpec((1, ATTN_BLOCK, D_ATTN), lambda bi, i: (bi, i, 0)),
                  pl.BlockSpec((1, lc, N_KVC), lambda bi, i: (bi, 0, 0))],
        out_specs=pl.BlockSpec((1, ATTN_BLOCK, D_ATTN), lambda bi, i: (bi, i, 0)),
        out_shape=jax.ShapeDtypeStruct((b, lc, D_ATTN), ACT),
        compiler_params=pltpu.CompilerParams(dimension_semantics=("parallel", "parallel"),
                                             vmem_limit_bytes=VMEM_LIMIT),
        name="attn_ctx",
    )(sink, qcx, kvx)


def _outproj_body(ya_ref, of_ref, ob_ref, yc_ref, zz_ref, x_ref, mod_ref, dnw_ref, w_ref, fnw_ref, o_ref, *, final):
    acc = jnp.dot(ya_ref[0].astype(BF16), w_ref[0:D_CONV, :], preferred_element_type=F32)
    o = of_ref[0].astype(F32) + ob_ref[0].astype(F32)
    for t in range(D_DN // LANES):
        ot = o[:, t * LANES:(t + 1) * LANES]
        ms = _head_sums(ot * ot) * (1.0 / HEAD_DIM)
        yb = ot * lax.rsqrt(ms + EPS) * dnw_ref[...] * _silu(zz_ref[0, :, t * LANES:(t + 1) * LANES].astype(F32))
        r0 = D_CONV + t * LANES
        acc = acc + jnp.dot(yb.astype(BF16), w_ref[r0:r0 + LANES, :], preferred_element_type=F32)
    yc = yc_ref[0].astype(F32) * _silu(zz_ref[0, :, D_DN:].astype(F32))
    acc = acc + jnp.dot(yc.astype(BF16), w_ref[D_CONV + D_DN:, :], preferred_element_type=F32)
    xn = x_ref[0] + mod_ref[0, 2:3, :] * acc
    if final:
        xn = xn * lax.rsqrt(jnp.mean(xn * xn, axis=-1, keepdims=True) + EPS) * fnw_ref[...]
    o_ref[0] = xn


def _outproj_call(ya, o_f, o_b, yc, zz, x, mod3, dnw, w_p, fnw, *, final, tm):
    b, l, d = x.shape
    tok = lambda n: pl.BlockSpec((1, tm, n), lambda bi, i: (bi, i, 0))
    return pl.pallas_call(
        functools.partial(_outproj_body, final=final),
        grid=(b, l // tm),
        in_specs=[tok(D_CONV), tok(D_DN), tok(D_DN), tok(D_ATTN), tok(N_ZZ), tok(d),
                  pl.BlockSpec((1, 3, d), lambda bi, i: (bi, 0, 0)),
                  pl.BlockSpec((1, LANES), lambda bi, i: (0, 0)),
                  pl.BlockSpec(w_p.shape, lambda bi, i: (0, 0)),
                  pl.BlockSpec((1, d), lambda bi, i: (0, 0))],
        out_specs=tok(d),
        out_shape=jax.ShapeDtypeStruct((b, l, d), F32),
        compiler_params=pltpu.CompilerParams(dimension_semantics=("parallel", "parallel"),
                                             vmem_limit_bytes=VMEM_LIMIT),
        name="outproj_final" if final else "outproj",
    )(ya, o_f, o_b, yc, zz, x, mod3, dnw, w_p, fnw)


def _permute_heads(w, axis):
    parts = jnp.split(w, N_Q_HEADS, axis=axis)
    return jnp.concatenate([parts[h] for h in Q_PERM], axis=axis)


def _layout_w_in(w):
    sizes = (D_CONV, D_CONV, D_CONV, D_DN, D_DN, D_DN, D_DN, N_SC, D_ATTN, D_KV, D_KV, D_ATTN)
    (a_val, a_gate, z_a, q_b, k_b, v_b, z_b, sc, q_c, k_c, v_c, z_c) = jnp.split(w, np.cumsum(sizes)[:-1].tolist(), axis=1)
    sc_pad = jnp.pad(sc, ((0, 0), (0, LANES - N_SC)))
    cols = [a_val, a_gate, z_a, q_b, k_b, v_b, sc_pad, _permute_heads(q_c, 1), k_c, v_c, z_b, _permute_heads(z_c, 1)]
    return jnp.concatenate(cols, axis=1).astype(BF16)


def _layout_w_out(w):
    return jnp.concatenate([w[:D_CONV + D_DN], _permute_heads(w[D_CONV + D_DN:], 0)], axis=0).astype(BF16)


def _lane_params(a_log, dt_bias):
    z = jnp.zeros((2, N_DN_HEADS), F32)
    one = jnp.ones((2, N_DN_HEADS), F32)
    rows = [jnp.stack([z, a_log], axis=1), jnp.stack([z, dt_bias], axis=1), jnp.stack([z, one], axis=1)]
    rows = jnp.stack([r.reshape(N_SC) for r in rows], axis=0)
    return jnp.pad(rows, ((0, 8 - rows.shape[0]), (0, LANES - N_SC)))


def _rope_tables(n_tok):
    quarter = HEAD_DIM // 4
    inv = ROPE_BASE ** (-jnp.arange(quarter, dtype=F32) / quarter)
    pos = jnp.arange(n_tok, dtype=jnp.int32)
    rows, cols = (pos // GRID_W).astype(F32), (pos % GRID_W).astype(F32)
    ang = jnp.concatenate([rows[:, None] * inv, rows[:, None] * inv, cols[:, None] * inv, cols[:, None] * inv], axis=1)
    sign = jnp.tile(jnp.concatenate([-jnp.ones(quarter, F32), jnp.ones(quarter, F32)]), 2)
    cos, sin = jnp.cos(ang), jnp.sin(ang) * sign
    return jnp.tile(cos, (1, LANES // HEAD_DIM)), jnp.tile(sin, (1, LANES // HEAD_DIM))


def _pick_tile(n, pref):
    t = min(pref, n)
    while n % t:
        t //= 2
    return t


def kernel(x, c, ctx, c_ctx, norm_w, ada_w, ada_b, w_in, conv_a_w, conv_a_b, ln_a_w, ln_a_b, qkv_conv_w, a_log,
           dt_bias, dn_norm_w, sink, w_out, final_norm_w):
    b, l, d = x.shape
    lc = ctx.shape[1]
    depth = w_in.shape[0]
    assert l % ATTN_BLOCK == 0 and lc % ATTN_BLOCK == 0 and l % GRID_W == 0

    rows = -(-(b + 1) // 8) * 8
    cond = jnp.zeros((rows, d), F32).at[:b].set(c).at[b].set(c_ctx)
    mod = _ada_call(cond, ada_w, ada_b).reshape(depth, rows, 3, d)
    cos_t, sin_t = _rope_tables(l)
    s_zero = jnp.zeros((b, 2, N_PAIRS, LANES, LANES), F32)
    tm = _pick_tile(l, 512)
    tmc = _pick_tile(lc, 256)

    xc = ctx
    for li in range(depth):
        last = li == depth - 1
        mod_x = mod[li, :b]
        mod_c = jnp.broadcast_to(mod[li, b][None], (b, 3, d))
        w_p = _layout_w_in(w_in[li])
        wo_p = _layout_w_out(w_out[li])
        lanep = _lane_params(a_log[li], dt_bias[li])
        nw = norm_w[li].reshape(1, d)
        dnw = jnp.tile(dn_norm_w[li], LANES // HEAD_DIM).reshape(1, LANES)
        fnw = final_norm_w.reshape(1, d)
        mix = (conv_a_w[li], conv_a_b[li], ln_a_w[li], ln_a_b[li], qkv_conv_w[li])

        pa, pqkv, gb, qc, kvc, zz = _inproj_call(x, mod_x, nw, w_p, lanep, cos_t, sin_t, rope=True, tm=tm)
        pa_x, pqkv_x, gb_x, qc_x, kvc_x, zz_x = _inproj_call(xc, mod_c, nw, w_p, lanep, cos_t[:lc], sin_t[:lc],
                                                             rope=False, tm=tmc)
        ya, qkvn = _mixprep_call(pa, pqkv, *mix, tile=_pick_tile(l, 512))
        ya_x, qkvn_x = _mixprep_call(pa_x, pqkv_x, *mix, tile=tmc)
        of_x, ob_x, s_ctx = _gdn_call(qkvn_x, gb_x, s_zero)
        o_f, o_b, _ = _gdn_call(qkvn, gb, s_ctx)
        yc = _attn_latent_call(qc, kvc, kvc_x, sink[li])
        x = _outproj_call(ya, o_f, o_b, yc, zz, x, mod_x, dnw, wo_p, fnw, final=last, tm=tm)
        if not last:
            yc_x = _attn_ctx_call(qc_x, kvc_x, sink[li])
            xc = _outproj_call(ya_x, of_x, ob_x, yc_x, zz_x, xc, mod_c, dnw, wo_p, fnw, final=False, tm=tmc)
    return x
```

```python
import functools

import jax
import jax.numpy as jnp
import numpy as np
from jax import lax
from jax.experimental import pallas as pl
from jax.experimental.pallas import tpu as pltpu

F32 = jnp.float32
BF16 = jnp.bfloat16
ACT = BF16

HEAD_DIM = 64
LANES = 128
SUBLANES = 8
D_CONV = 256
CONV_WIDTH = 31
N_DN_HEADS = 6
D_DN = N_DN_HEADS * HEAD_DIM
SHORT_CONV = 5
DN_CHUNK = 64
N_Q_HEADS = 6
N_KV_HEADS = 2
D_ATTN = N_Q_HEADS * HEAD_DIM
D_KV = N_KV_HEADS * HEAD_DIM
WINDOW = 128
ATTN_BLOCK = 128
ROPE_BASE = 10000.0
GRID_W = 64
EPS = 1e-6
N_PAIRS = N_DN_HEADS // 2
HALO = 16
NEG = -1e30
VMEM_LIMIT = 48 * 1024 * 1024
MIX_ROWS = 128
ATTN_BLOCKS_PER_STEP = 4
GDN_BATCH_ROWS = 4

N_A = 3 * D_CONV
N_QKV = 3 * D_DN
N_SC = 4 * N_DN_HEADS
N_KVC = 2 * D_KV
N_ZZ = D_DN + D_ATTN
OFF_QKV = N_A
OFF_SC = OFF_QKV + N_QKV
OFF_QC = OFF_SC + LANES
OFF_KVC = OFF_QC + D_ATTN
OFF_ZZ = OFF_KVC + N_KVC
N_IN_PAD = OFF_ZZ + N_ZZ
Q_PERM = (0, 3, 1, 4, 2, 5)


def _silu(x):
    return x * jax.nn.sigmoid(x)


def _mm(a, b):
    return jnp.dot(a.astype(BF16), b.astype(BF16), preferred_element_type=F32)


def _mm_nt(a, b):
    return lax.dot_general(a.astype(BF16), b.astype(BF16), (((1,), (1,)), ((), ())), preferred_element_type=F32)


def _mm_tn(a, b):
    return lax.dot_general(a.astype(BF16), b.astype(BF16), (((0,), (0,)), ((), ())), preferred_element_type=F32)


def _split2(x):
    hi = x.astype(BF16)
    lo = (x - hi.astype(F32)).astype(BF16)
    return hi, lo


def _mm_x2(a, b01):
    hi, lo = _split2(a)
    return (jnp.dot(hi, b01, preferred_element_type=F32) + jnp.dot(lo, b01, preferred_element_type=F32))


def _head_sums(x):
    r = lax.broadcasted_iota(jnp.int32, (LANES, LANES), 0) // HEAD_DIM
    c = lax.broadcasted_iota(jnp.int32, (LANES, LANES), 1) // HEAD_DIM
    ones_bd = jnp.where(r == c, 1.0, 0.0).astype(BF16)
    return _mm_x2(x, ones_bd)


def _ada_body(cond_ref, w_ref, b_ref, o_ref):
    a = _silu(cond_ref[...])
    o_ref[0] = jnp.dot(a, w_ref[0], preferred_element_type=F32, precision=lax.Precision.HIGHEST) + b_ref[0]


def _ada_call(cond, ada_w, ada_b):
    depth, d, d3 = ada_w.shape
    rows = cond.shape[0]
    tn = 1024
    return pl.pallas_call(
        _ada_body,
        grid=(depth, d3 // tn),
        in_specs=[pl.BlockSpec((rows, d), lambda l, j: (0, 0)),
                  pl.BlockSpec((1, d, tn), lambda l, j: (l, 0, j)),
                  pl.BlockSpec((1, 1, tn), lambda l, j: (l, 0, j))],
        out_specs=pl.BlockSpec((1, rows, tn), lambda l, j: (l, 0, j)),
        out_shape=jax.ShapeDtypeStruct((depth, rows, d3), F32),
        compiler_params=pltpu.CompilerParams(vmem_limit_bytes=VMEM_LIMIT),
        name="ada",
    )(cond, ada_w, ada_b.reshape(depth, 1, d3))


def _inproj_body(x_ref, mod_ref, nw_ref, w_ref, lanep_ref, cos_ref, sin_ref,
                 pa_ref, pqkv_ref, gb_ref, qc_ref, kvc_ref, zz_ref, *, rope):
    x = x_ref[0]
    y = x * lax.rsqrt(jnp.mean(x * x, axis=-1, keepdims=True) + EPS) * nw_ref[...]
    h = (y * (1.0 + mod_ref[0, 1:2, :]) + mod_ref[0, 0:1, :]).astype(BF16)

    def proj(off, n):
        return jnp.dot(h, w_ref[:, off:off + n], preferred_element_type=F32)

    pa_ref[0] = proj(0, N_A).astype(ACT)
    pqkv_ref[0] = proj(OFF_QKV, N_QKV).astype(ACT)
    zz_ref[0] = proj(OFF_ZZ, N_ZZ).astype(ACT)

    s = proj(OFF_SC, LANES)
    a_log, dt_b, is_decay = lanep_ref[0:1, :], lanep_ref[1:2, :], lanep_ref[2:3, :] > 0.5
    t = s + dt_b
    softplus = jnp.maximum(t, 0.0) + jnp.log(1.0 + jnp.exp(-jnp.abs(t)))
    gb_ref[0] = jnp.where(is_decay, -jnp.exp(a_log) * softplus, jax.nn.sigmoid(s))

    qk = proj(OFF_QC, D_ATTN + D_KV)
    v = proj(OFF_KVC + D_KV, D_KV)
    lane = lax.broadcasted_iota(jnp.int32, (1, LANES), 1)
    first = (lane % (HEAD_DIM // 2)) < (HEAD_DIM // 4)
    for t_ in range((D_ATTN + D_KV) // LANES):
        tile = qk[:, t_ * LANES:(t_ + 1) * LANES]
        if rope:
            partner = jnp.where(first, pltpu.roll(tile, LANES - HEAD_DIM // 4, 1), pltpu.roll(tile, HEAD_DIM // 4, 1))
            tile = tile * cos_ref[...] + partner * sin_ref[...]
        if t_ < D_ATTN // LANES:
            qc_ref[0, :, t_ * LANES:(t_ + 1) * LANES] = (tile * (HEAD_DIM ** -0.5)).astype(ACT)
        else:
            kvc_ref[0, :, 0:D_KV] = tile.astype(ACT)
    kvc_ref[0, :, D_KV:] = v.astype(ACT)


def _inproj_call(x, mod3, norm_w, w_p, lanep, cos_t, sin_t, *, rope, tm):
    b, l, d = x.shape
    outs = (N_A, N_QKV, LANES, D_ATTN, N_KVC, N_ZZ)
    tok = lambda n: pl.BlockSpec((1, tm, n), lambda bi, i: (bi, i, 0))
    return pl.pallas_call(
        functools.partial(_inproj_body, rope=rope),
        grid=(b, l // tm),
        in_specs=[tok(d),
                  pl.BlockSpec((1, 3, d), lambda bi, i: (bi, 0, 0)),
                  pl.BlockSpec((1, d), lambda bi, i: (0, 0)),
                  pl.BlockSpec((d, N_IN_PAD), lambda bi, i: (0, 0)),
                  pl.BlockSpec((8, LANES), lambda bi, i: (0, 0)),
                  pl.BlockSpec((tm, LANES), lambda bi, i: (i, 0)),
                  pl.BlockSpec((tm, LANES), lambda bi, i: (i, 0))],
        out_specs=[tok(n) for n in outs],
        out_shape=[jax.ShapeDtypeStruct((b, l, n), F32 if n == LANES else ACT) for n in outs],
        compiler_params=pltpu.CompilerParams(dimension_semantics=("parallel", "parallel"),
                                             vmem_limit_bytes=VMEM_LIMIT),
        name="inproj_rope" if rope else "inproj",
    )(x, mod3, norm_w, w_p, lanep, cos_t, sin_t)


def _fill_ext(ext_ref, main, prev, nxt, i, n_tiles, tile):
    ext_ref[0:HALO, :] = jnp.where(i > 0, prev, 0.0)
    ext_ref[HALO:HALO + tile, :] = main
    ext_ref[HALO + tile:, :] = jnp.where(i < n_tiles - 1, nxt, 0.0)


def _dwconv_shifted(sh_ref, w_ref, width, rows, r0):
    pad = (width - 1) // 2
    acc = None
    for k in range(width):
        off = r0 + HALO - pad + k
        r = off % SUBLANES
        term = sh_ref[r, off - r:off - r + rows, :] * w_ref[k:k + 1, :]
        acc = term if acc is None else acc + term
    return acc


def _mixprep_body(pa_ref, pap_ref, pan_ref, q_ref, qp_ref, qn_ref, caw_ref, cab_ref, lnw_ref, lnb_ref, qcw_ref,
                  ya_ref, qkvn_ref, exta_ref, extq_ref, *, tile, n_tiles):
    i = pl.program_id(1)
    glu = lambda t: t[:, 0:D_CONV].astype(F32) * jax.nn.sigmoid(t[:, D_CONV:2 * D_CONV].astype(F32))
    _fill_ext(exta_ref.at[0], glu(pa_ref[0]), glu(pap_ref[0]), glu(pan_ref[0]), i, n_tiles, tile)
    _fill_ext(extq_ref, q_ref[0], qp_ref[0], qn_ref[0], i, n_tiles, tile)
    n_sh = tile + 2 * HALO - SUBLANES
    for r in range(1, SUBLANES):
        exta_ref[r, 0:n_sh, :] = exta_ref[0, r:r + n_sh, :]
    sub = min(MIX_ROWS, tile)

    def conformer_block(r0):
        u = _dwconv_shifted(exta_ref, caw_ref, CONV_WIDTH, sub, r0) + cab_ref[...]
        mu = jnp.mean(u, axis=-1, keepdims=True)
        var = jnp.mean(jnp.square(u - mu), axis=-1, keepdims=True)
        u = (u - mu) * lax.rsqrt(var + EPS) * lnw_ref[...] + lnb_ref[...]
        ya_ref[0, r0:r0 + sub, :] = (_silu(u) * _silu(pa_ref[0, r0:r0 + sub, 2 * D_CONV:].astype(F32))).astype(ACT)

    win = sub + 2 * HALO
    lead = HALO - (SHORT_CONV - 1) // 2
    assert sub + lead + SHORT_CONV - 1 <= win and ACT == BF16
    sr = lax.broadcasted_iota(jnp.int32, (SHORT_CONV * sub, win), 0)
    sc = lax.broadcasted_iota(jnp.int32, (SHORT_CONV * sub, win), 1)
    shift = jnp.where(sc == (sr % sub) + lead + sr // sub, 1.0, 0.0).astype(ACT)
    shifted = lambda r0: jnp.dot(shift, extq_ref[r0:r0 + win, :], preferred_element_type=F32)
    nxt_taps = shifted(0)
    for r0 in range(0, tile, sub):
        taps = nxt_taps
        if r0 + sub < tile:
            nxt_taps = shifted(r0 + sub)
        conformer_block(r0)
        acc = None
        for k in range(SHORT_CONV):
            term = taps[k * sub:(k + 1) * sub] * qcw_ref[k:k + 1, :]
            acc = term if acc is None else acc + term
        t = _silu(acc)
        for c0 in range(0, N_QKV, LANES):
            tl = t[:, c0:c0 + LANES]
            if c0 < 2 * D_DN:
                tl = tl * lax.rsqrt(_head_sums(tl * tl) + EPS)
                if c0 < D_DN:
                    tl = tl * (HEAD_DIM ** -0.5)
            qkvn_ref[0, r0:r0 + sub, c0:c0 + LANES] = tl.astype(ACT)


def _mixprep_call(pa, pqkv, conv_a_w, conv_a_b, ln_w, ln_b, qkv_conv_w, *, tile):
    b, l, _ = pa.shape
    n_tiles = l // tile
    hpt = tile // HALO
    n_h = l // HALO
    main = lambda n: pl.BlockSpec((1, tile, n), lambda bi, i: (bi, i, 0))
    prev = lambda n: pl.BlockSpec((1, HALO, n), lambda bi, i: (bi, jnp.maximum(i * hpt - 1, 0), 0))
    nxt = lambda n: pl.BlockSpec((1, HALO, n), lambda bi, i: (bi, jnp.minimum((i + 1) * hpt, n_h - 1), 0))
    full = lambda a: pl.BlockSpec(a.shape, lambda bi, i: (0, 0))
    consts = (conv_a_w, conv_a_b.reshape(1, -1), ln_w.reshape(1, -1), ln_b.reshape(1, -1), qkv_conv_w)
    return pl.pallas_call(
        functools.partial(_mixprep_body, tile=tile, n_tiles=n_tiles),
        grid=(b, n_tiles),
        in_specs=[main(N_A), prev(N_A), nxt(N_A), main(N_QKV), prev(N_QKV), nxt(N_QKV)] + [full(a) for a in consts],
        out_specs=[main(D_CONV), main(N_QKV)],
        out_shape=[jax.ShapeDtypeStruct((b, l, D_CONV), ACT), jax.ShapeDtypeStruct((b, l, N_QKV), ACT)],
        scratch_shapes=[pltpu.VMEM((SUBLANES, tile + 2 * HALO, D_CONV), F32),
                        pltpu.VMEM((tile + 2 * HALO, N_QKV), ACT)],
        compiler_params=pltpu.CompilerParams(dimension_semantics=("parallel", "parallel"),
                                             vmem_limit_bytes=VMEM_LIMIT),
        name="mixprep",
    )(pa, pa, pa, pqkv, pqkv, pqkv, *consts)


def _blockdiag(x2, bm):
    return jnp.where(bm, jnp.concatenate([x2, x2], axis=0), 0.0)


def _gdn_chunk(dirs, s_ref):
    c = DN_CHUNK
    ri = lax.broadcasted_iota(jnp.int32, (c, LANES), 0)
    cj = lax.broadcasted_iota(jnp.int32, (c, LANES), 1) % HEAD_DIM
    blk16 = (ri // 16) == (cj // 16)
    blk32 = (ri // 32) == (cj // 32)
    eye = jnp.where(ri == cj, 1.0, 0.0)
    bm = (lax.broadcasted_iota(jnp.int32, (LANES, LANES), 0) // HEAD_DIM
          == lax.broadcasted_iota(jnp.int32, (LANES, LANES), 1) // HEAD_DIM)
    ti = lax.broadcasted_iota(jnp.int32, (c, c), 0)
    tt = lax.broadcasted_iota(jnp.int32, (c, c), 1)
    feat = lax.broadcasted_iota(jnp.int32, (1, LANES), 1)
    is_decay = (feat % (2 * N_DN_HEADS)) >= N_DN_HEADS
    lane_lo = feat < HEAD_DIM
    tn = (((0,), (0,)), ((), ()))

    per_dir = {}
    for bi, d, qkv, gb, o_ref in dirs:
        tri = jnp.where((tt <= ti) if d == 0 else (tt >= ti), 1.0, 0.0).astype(BF16)
        tri_t2 = jnp.where((ri <= cj) if d == 0 else (ri >= cj), 1.0, 0.0).astype(BF16)
        g_hi, g_lo = _split2(gb)
        csum = jnp.dot(tri, g_hi, preferred_element_type=F32) + jnp.dot(tri, g_lo, preferred_element_type=F32)
        crow = (lax.dot_general(g_hi, tri_t2, tn, preferred_element_type=F32)
                + lax.dot_general(g_lo, tri_t2, tn, preferred_element_type=F32))
        src = jnp.where(is_decay, csum, gb)
        incl = (ri >= cj) if d == 0 else (ri <= cj)
        strict = (ri > cj) if d == 0 else (ri < cj)
        per_dir[bi, d] = (qkv, o_ref, crow, src, incl, strict)

    chains = [(bi, d, p) for bi, d, _, _, _ in dirs for p in range(N_PAIRS)]
    each = lambda f, *cols: [f(*args) for args in zip(*cols)]

    def load(bi, d, p):
        qkv = per_dir[bi, d][0]
        return tuple(qkv[:, part * D_DN + p * LANES:part * D_DN + (p + 1) * LANES] for part in range(3))

    q_in, k_in, v_in = zip(*[load(*ch) for ch in chains])
    q2, k2, v2 = ([t.astype(F32) for t in col] for col in (q_in, k_in, v_in))

    def spread(bi, d, p, kind):
        src = per_dir[bi, d][3]
        f = d * 2 * N_DN_HEADS + kind * N_DN_HEADS + 2 * p
        return jnp.where(lane_lo, src[:, f:f + 1], src[:, f + 1:f + 2])

    beta2 = [spread(*ch, 0) for ch in chains]
    c2 = [spread(*ch, 1) for ch in chains]

    def decay(ch, c2_):
        bi, d, p = ch
        crow, incl = per_dir[bi, d][2], per_dir[bi, d][4]
        f0 = d * 2 * N_DN_HEADS + N_DN_HEADS + 2 * p
        crow2 = jnp.where(lane_lo, crow[f0:f0 + 1, :], crow[f0 + 1:f0 + 2, :])
        return jnp.exp(jnp.where(incl, c2_ - crow2, NEG))

    dec = each(decay, chains, c2)
    tot = [c2_[c - 1:c, :] if d == 0 else c2_[0:1, :] for (_, d, _), c2_ in zip(chains, c2)]
    ec2 = [jnp.exp(t) for t in c2]
    gq = each(lambda k, q: _mm_nt(jnp.concatenate([k, q], axis=0), _blockdiag(k, bm)), k_in, q_in)
    m2 = [jnp.where(per_dir[bi, d][5], -(g[:c] * b_ * dc), 0.0)
          for (bi, d, _), g, b_, dc in zip(chains, gq, beta2, dec)]
    intra = [g[c:] * dc for g, dc in zip(gq, dec)]

    pmul = lambda a2, b2: _mm(a2, _blockdiag(b2, bm))
    md = [jnp.where(blk16, m, 0.0) for m in m2]
    tinv = [eye + m for m in md]
    pw = each(pmul, md, md)
    for _ in range(2):
        both = each(lambda t, x: pmul(jnp.concatenate([t, x], axis=0), x), tinv, pw)
        tinv = each(lambda t, y: t + y[:c], tinv, both)
        pw = [y[c:] for y in both]
    tinv = each(lambda t, x: t + pmul(t, x), tinv, pw)
    for off in ([jnp.where(blk32 & ~blk16, m, 0.0) for m in m2], [jnp.where(blk32, 0.0, m) for m in m2]):
        half = each(pmul, tinv, off)
        tinv = each(lambda t, h: t + pmul(h, t), tinv, half)
    rhs = each(lambda v, k, b_, e: jnp.concatenate([_blockdiag(v * b_, bm), _blockdiag(k * b_ * e, bm)], axis=1),
               v2, k2, beta2, ec2)
    uw = each(_mm, tinv, rhs)
    kdec = each(lambda k, t, c2_: k * jnp.exp(t - c2_), k2, tot, c2)
    s_old = [s_ref[bi, d, p] for bi, d, p in chains]
    wq = each(lambda u, q, e, s: _mm(jnp.concatenate([u[:, LANES:], q * e], axis=0), s), uw, q2, ec2, s_old)
    vnew = each(lambda u, w: u[:, :LANES] - w[:c], uw, wq)
    o2 = each(lambda w, a, v: w[c:] + _mm(a, _blockdiag(v, bm)), wq, intra, vnew)
    s_new = each(lambda s, t, k, v: s * jnp.exp(t) + jnp.where(bm, _mm_tn(k, v), 0.0), s_old, tot, kdec, vnew)
    for (bi, d, p), o, s in zip(chains, o2, s_new):
        per_dir[bi, d][1][bi, :, p * LANES:(p + 1) * LANES] = o.astype(ACT)
        s_ref[bi, d, p] = s


def _gdn_body(qkvf_ref, qkvb_ref, gbf_ref, gbb_ref, s0_ref, of_ref, ob_ref, sfin_ref, s_ref, *, n_chunks, bb):
    n = pl.program_id(1)

    @pl.when(n == 0)
    def _():
        s_ref[...] = s0_ref[...]

    dirs = []
    for bi in range(bb):
        dirs += [(bi, 0, qkvf_ref[bi], gbf_ref[bi], of_ref), (bi, 1, qkvb_ref[bi], gbb_ref[bi], ob_ref)]
    _gdn_chunk(dirs, s_ref)

    @pl.when(n == n_chunks - 1)
    def _():
        sfin_ref[...] = s_ref[...]


def _gdn_call(qkvn, gb, s0):
    b, l, _ = qkvn.shape
    c = DN_CHUNK
    nc = l // c
    bb = _pick_tile(b, GDN_BATCH_ROWS)
    fwd = lambda n_: pl.BlockSpec((bb, c, n_), lambda bi, n: (bi, n, 0))
    bwd = lambda n_: pl.BlockSpec((bb, c, n_), lambda bi, n: (bi, nc - 1 - n, 0))
    st = pl.BlockSpec((bb, 2, N_PAIRS, LANES, LANES), lambda bi, n: (bi, 0, 0, 0, 0))
    return pl.pallas_call(
        functools.partial(_gdn_body, n_chunks=nc, bb=bb),
        grid=(b // bb, nc),
        in_specs=[fwd(N_QKV), bwd(N_QKV), fwd(LANES), bwd(LANES), st],
        out_specs=[fwd(D_DN), bwd(D_DN), st],
        out_shape=[jax.ShapeDtypeStruct((b, l, D_DN), ACT), jax.ShapeDtypeStruct((b, l, D_DN), ACT),
                   jax.ShapeDtypeStruct(s0.shape, F32)],
        scratch_shapes=[pltpu.VMEM((bb, 2, N_PAIRS, LANES, LANES), F32)],
        compiler_params=pltpu.CompilerParams(dimension_semantics=("parallel", "arbitrary"),
                                             vmem_limit_bytes=VMEM_LIMIT),
        name="gdn",
    )(qkvn, qkvn, gb, gb, s0)


def _attend(problems, sink_ref):
    lane_lo = lax.broadcasted_iota(jnp.int32, (1, LANES), 1) < HEAD_DIM

    def stack_heads(q):
        return jnp.concatenate(
            [jnp.where(lane_lo if half == 0 else ~lane_lo, q[:, t * LANES:(t + 1) * LANES], 0.0).astype(BF16)
             for t in range(D_ATTN // LANES) for half in range(2)], axis=0)

    scores = [_mm_nt(stack_heads(q), keys) for q, keys, _, _ in problems]
    soft = []
    for (q, _, _, mask), s_all in zip(problems, scores):
        nq = q.shape[0]
        probs, dens = [], []
        for j in range(N_Q_HEADS):
            sink = sink_ref[Q_PERM[j]]
            s = s_all[j * nq:(j + 1) * nq]
            if mask is not None:
                s = jnp.where(mask, s, NEG)
            m = jnp.maximum(jnp.max(s, axis=-1, keepdims=True), sink)
            pr = jnp.exp(s - m)
            dens.append(jnp.sum(pr, axis=-1, keepdims=True) + jnp.exp(sink - m))
            probs.append(pr.astype(BF16))
        soft.append((jnp.concatenate(probs, axis=0), dens))
    outs = []
    for (q, _, vals, _), (probs, dens) in zip(problems, soft):
        nq = q.shape[0]
        o_all = _mm(probs, vals)
        outs.append([jnp.where(lane_lo, o_all[2 * t * nq:(2 * t + 1) * nq] / dens[2 * t],
                               o_all[(2 * t + 1) * nq:(2 * t + 2) * nq] / dens[2 * t + 1])
                     for t in range(D_ATTN // LANES)])
    return outs


def _attn_latent_body(sink_ref, q_ref, kvp_ref, kvc_ref, kvn_ref, kvx_ref, o_ref, *, n_steps, sub):
    i = pl.program_id(1)
    blk = ATTN_BLOCK
    lc = kvx_ref.shape[1]
    band = jnp.concatenate([kvp_ref[0], kvc_ref[0], kvn_ref[0]], axis=0)
    ctx = kvx_ref[0]
    t = lax.broadcasted_iota(jnp.int32, (blk, 3 * blk + lc), 0)
    u = lax.broadcasted_iota(jnp.int32, (blk, 3 * blk + lc), 1)
    problems = []
    for j in range(sub):
        kv = jnp.concatenate([band[j * blk:(j + 3) * blk], ctx], axis=0)
        lo = jnp.where(i > 0, 0, blk) if j == 0 else 0
        hi = jnp.where(i < n_steps - 1, 3 * blk, 2 * blk) if j == sub - 1 else 3 * blk
        valid = (u >= 3 * blk) | ((jnp.abs(u - blk - t) <= WINDOW) & (u >= lo) & (u < hi))
        problems.append((q_ref[0, j * blk:(j + 1) * blk, :], kv[:, :D_KV], kv[:, D_KV:], valid))
    for j, tiles in enumerate(_attend(problems, sink_ref)):
        for t_, tile in enumerate(tiles):
            o_ref[0, j * blk:(j + 1) * blk, t_ * LANES:(t_ + 1) * LANES] = tile.astype(ACT)


def _attn_ctx_body(sink_ref, q_ref, kvx_ref, o_ref):
    kv = kvx_ref[0]
    (tiles,) = _attend([(q_ref[0], kv[:, :D_KV], kv[:, D_KV:], None)], sink_ref)
    for t_, tile in enumerate(tiles):
        o_ref[0, :, t_ * LANES:(t_ + 1) * LANES] = tile.astype(ACT)


def _attn_latent_call(qc, kvc, kvx, sink):
    b, l, _ = qc.shape
    lc = kvx.shape[1]
    blk = ATTN_BLOCK
    nb = l // blk
    sub = _pick_tile(nb, ATTN_BLOCKS_PER_STEP)
    edge = lambda f: pl.BlockSpec((1, blk, N_KVC), lambda bi, i: (bi, f(i), 0))
    return pl.pallas_call(
        functools.partial(_attn_latent_body, n_steps=nb // sub, sub=sub),
        grid=(b, nb // sub),
        in_specs=[pl.BlockSpec(memory_space=pltpu.SMEM),
                  pl.BlockSpec((1, sub * blk, D_ATTN), lambda bi, i: (bi, i, 0)),
                  edge(lambda i: jnp.maximum(sub * i - 1, 0)),
                  pl.BlockSpec((1, sub * blk, N_KVC), lambda bi, i: (bi, i, 0)),
                  edge(lambda i: jnp.minimum(sub * (i + 1), nb - 1)),
                  pl.BlockSpec((1, lc, N_KVC), lambda bi, i: (bi, 0, 0))],
        out_specs=pl.BlockSpec((1, sub * blk, D_ATTN), lambda bi, i: (bi, i, 0)),
        out_shape=jax.ShapeDtypeStruct((b, l, D_ATTN), ACT),
        compiler_params=pltpu.CompilerParams(dimension_semantics=("parallel", "parallel"),
                                             vmem_limit_bytes=VMEM_LIMIT),
        name="attn_latent",
    )(sink, qc, kvc, kvc, kvc, kvx)


def _attn_ctx_call(qcx, kvx, sink):
    b, lc, _ = qcx.shape
    return pl.pallas_call(
        _attn_ctx_body,
        grid=(b, lc // ATTN_BLOCK),
        in_specs=[pl.BlockSpec(memory_space=pltpu.SMEM),
                  pl.BlockSpec((1, ATTN_BLOCK, D_ATTN), lambda bi, i: (bi, i, 0)),
                  pl.BlockSpec((1, lc, N_KVC), lambda bi, i: (bi, 0, 0))],
        out_specs=pl.BlockSpec((1, ATTN_BLOCK, D_ATTN), lambda bi, i: (bi, i, 0)),
        out_shape=jax.ShapeDtypeStruct((b, lc, D_ATTN), ACT),
        compiler_params=pltpu.CompilerParams(dimension_semantics=("parallel", "parallel"),
                                             vmem_limit_bytes=VMEM_LIMIT),
        name="attn_ctx",
    )(sink, qcx, kvx)


def _outproj_body(ya_ref, of_ref, ob_ref, yc_ref, zz_ref, x_ref, mod_ref, dnw_ref, w_ref, fnw_ref, o_ref, *, final):
    parts = [ya_ref[0].astype(BF16)]
    o = of_ref[0].astype(F32) + ob_ref[0].astype(F32)
    for t in range(D_DN // LANES):
        ot = o[:, t * LANES:(t + 1) * LANES]
        ms = _head_sums(ot * ot) * (1.0 / HEAD_DIM)
        yb = ot * lax.rsqrt(ms + EPS) * dnw_ref[...] * _silu(zz_ref[0, :, t * LANES:(t + 1) * LANES].astype(F32))
        parts.append(yb.astype(BF16))
    yc = yc_ref[0].astype(F32) * _silu(zz_ref[0, :, D_DN:].astype(F32))
    parts.append(yc.astype(BF16))
    acc = jnp.dot(jnp.concatenate(parts, axis=1), w_ref[...], preferred_element_type=F32)
    xn = x_ref[0] + mod_ref[0, 2:3, :] * acc
    if final:
        xn = xn * lax.rsqrt(jnp.mean(xn * xn, axis=-1, keepdims=True) + EPS) * fnw_ref[...]
    o_ref[0] = xn


def _outproj_call(ya, o_f, o_b, yc, zz, x, mod3, dnw, w_p, fnw, *, final, tm):
    b, l, d = x.shape
    tok = lambda n: pl.BlockSpec((1, tm, n), lambda bi, i: (bi, i, 0))
    return pl.pallas_call(
        functools.partial(_outproj_body, final=final),
        grid=(b, l // tm),
        in_specs=[tok(D_CONV), tok(D_DN), tok(D_DN), tok(D_ATTN), tok(N_ZZ), tok(d),
                  pl.BlockSpec((1, 3, d), lambda bi, i: (bi, 0, 0)),
                  pl.BlockSpec((1, LANES), lambda bi, i: (0, 0)),
                  pl.BlockSpec(w_p.shape, lambda bi, i: (0, 0)),
                  pl.BlockSpec((1, d), lambda bi, i: (0, 0))],
        out_specs=tok(d),
        out_shape=jax.ShapeDtypeStruct((b, l, d), F32),
        compiler_params=pltpu.CompilerParams(dimension_semantics=("parallel", "parallel"),
                                             vmem_limit_bytes=VMEM_LIMIT),
        name="outproj_final" if final else "outproj",
    )(ya, o_f, o_b, yc, zz, x, mod3, dnw, w_p, fnw)


def _permute_heads(w, axis):
    parts = jnp.split(w, N_Q_HEADS, axis=axis)
    return jnp.concatenate([parts[h] for h in Q_PERM], axis=axis)


def _layout_w_in(w):
    sizes = (D_CONV, D_CONV, D_CONV, D_DN, D_DN, D_DN, D_DN, N_SC, D_ATTN, D_KV, D_KV, D_ATTN)
    (a_val, a_gate, z_a, q_b, k_b, v_b, z_b, sc, q_c, k_c, v_c, z_c) = jnp.split(w, np.cumsum(sizes)[:-1].tolist(), axis=1)
    sc_pad = jnp.pad(sc, ((0, 0), (0, LANES - N_SC)))
    cols = [a_val, a_gate, z_a, q_b, k_b, v_b, sc_pad, _permute_heads(q_c, 1), k_c, v_c, z_b, _permute_heads(z_c, 1)]
    return jnp.concatenate(cols, axis=1).astype(BF16)


def _layout_w_out(w):
    return jnp.concatenate([w[:D_CONV + D_DN], _permute_heads(w[D_CONV + D_DN:], 0)], axis=0).astype(BF16)


def _lane_params(a_log, dt_bias):
    z = jnp.zeros((2, N_DN_HEADS), F32)
    one = jnp.ones((2, N_DN_HEADS), F32)
    rows = [jnp.stack([z, a_log], axis=1), jnp.stack([z, dt_bias], axis=1), jnp.stack([z, one], axis=1)]
    rows = jnp.stack([r.reshape(N_SC) for r in rows], axis=0)
    return jnp.pad(rows, ((0, 8 - rows.shape[0]), (0, LANES - N_SC)))


def _rope_tables(n_tok):
    quarter = HEAD_DIM // 4
    inv = ROPE_BASE ** (-jnp.arange(quarter, dtype=F32) / quarter)
    pos = jnp.arange(n_tok, dtype=jnp.int32)
    rows, cols = (pos // GRID_W).astype(F32), (pos % GRID_W).astype(F32)
    ang = jnp.concatenate([rows[:, None] * inv, rows[:, None] * inv, cols[:, None] * inv, cols[:, None] * inv], axis=1)
    sign = jnp.tile(jnp.concatenate([-jnp.ones(quarter, F32), jnp.ones(quarter, F32)]), 2)
    cos, sin = jnp.cos(ang), jnp.sin(ang) * sign
    return jnp.tile(cos, (1, LANES // HEAD_DIM)), jnp.tile(sin, (1, LANES // HEAD_DIM))


def _pick_tile(n, pref):
    t = min(pref, n)
    while n % t:
        t //= 2
    return t


def kernel(x, c, ctx, c_ctx, norm_w, ada_w, ada_b, w_in, conv_a_w, conv_a_b, ln_a_w, ln_a_b, qkv_conv_w, a_log,
           dt_bias, dn_norm_w, sink, w_out, final_norm_w):
    b, l, d = x.shape
    lc = ctx.shape[1]
    depth = w_in.shape[0]
    assert l % ATTN_BLOCK == 0 and lc % ATTN_BLOCK == 0 and l % GRID_W == 0

    rows = -(-(b + 1) // 8) * 8
    cond = jnp.zeros((rows, d), F32).at[:b].set(c).at[b].set(c_ctx)
    mod = _ada_call(cond, ada_w, ada_b).reshape(depth, rows, 3, d)
    cos_t, sin_t = _rope_tables(l)
    s_zero = jnp.zeros((b, 2, N_PAIRS, LANES, LANES), F32)
    tm = _pick_tile(l, 512)
    tmc = _pick_tile(lc, 256)

    xc = ctx
    for li in range(depth):
        last = li == depth - 1
        mod_x = mod[li, :b]
        mod_c = jnp.broadcast_to(mod[li, b][None], (b, 3, d))
        w_p = _layout_w_in(w_in[li])
        wo_p = _layout_w_out(w_out[li])
        lanep = _lane_params(a_log[li], dt_bias[li])
        nw = norm_w[li].reshape(1, d)
        dnw = jnp.tile(dn_norm_w[li], LANES // HEAD_DIM).reshape(1, LANES)
        fnw = final_norm_w.reshape(1, d)
        mix = (conv_a_w[li], conv_a_b[li], ln_a_w[li], ln_a_b[li], qkv_conv_w[li])

        pa, pqkv, gb, qc, kvc, zz = _inproj_call(x, mod_x, nw, w_p, lanep, cos_t, sin_t, rope=True, tm=tm)
        pa_x, pqkv_x, gb_x, qc_x, kvc_x, zz_x = _inproj_call(xc, mod_c, nw, w_p, lanep, cos_t[:lc], sin_t[:lc],
                                                             rope=False, tm=tmc)
        ya, qkvn = _mixprep_call(pa, pqkv, *mix, tile=_pick_tile(l, 512))
        ya_x, qkvn_x = _mixprep_call(pa_x, pqkv_x, *mix, tile=tmc)
        of_x, ob_x, s_ctx = _gdn_call(qkvn_x, gb_x, s_zero)
        o_f, o_b, _ = _gdn_call(qkvn, gb, s_ctx)
        yc = _attn_latent_call(qc, kvc, kvc_x, sink[li])
        x = _outproj_call(ya, o_f, o_b, yc, zz, x, mod_x, dnw, wo_p, fnw, final=last, tm=tm)
        if not last:
            yc_x = _attn_ctx_call(qc_x, kvc_x, sink[li])
            xc = _outproj_call(ya_x, of_x, ob_x, yc_x, zz_x, xc, mod_c, dnw, wo_p, fnw, final=False, tm=tmc)
    return x
```
